```python
import math
import jax, jax.numpy as jnp
from jax import lax
import numpy as np

D_MODEL = 1024
BATCH = 16
SEQ = 4096
DEPTH = 2
DEC_BATCH = 2
DEC_SEQ = 16384
PAST_LEN = 128

HG_HEADS = 4
HG_DK = 128
HG_DV = 128
HG_CHUNK = 64
DA_HEADS = 4
DA_DH = 64
DA_Q_BLOCK = 128
ROPE_THETA = 500000.0
ROPE_DIM = DA_DH // 4
NORM_EPS = 1e-6
SUBLN_EPS = 1e-5

HG_QK_WIDTH = HG_HEADS * HG_DK
HG_V_WIDTH = HG_HEADS * HG_DV
DA_QK_WIDTH = DA_HEADS * 2 * DA_DH
DA_V_WIDTH = DA_HEADS * 2 * DA_DH
IN_WIDTHS = (HG_QK_WIDTH, HG_QK_WIDTH, HG_QK_WIDTH, HG_V_WIDTH, HG_V_WIDTH,
             DA_QK_WIDTH, DA_QK_WIDTH, DA_V_WIDTH, DA_V_WIDTH, D_MODEL, D_MODEL)
D_IN = 6656

kernel_name = "hybrid_hgrn2_diffattn_gated_encoder"


def rms_norm(x, gain, eps=NORM_EPS):
    xf = x.astype(jnp.float32)
    y = xf * lax.rsqrt(jnp.mean(xf * xf, axis=-1, keepdims=True) + eps)
    return (y * gain.astype(jnp.float32)).astype(x.dtype)


def rotary_partial(t, positions):
    half = ROPE_DIM // 2
    inv_freq = jnp.power(ROPE_THETA, -jnp.arange(0, ROPE_DIM, 2, dtype=jnp.float32) / ROPE_DIM)
    ang = positions.astype(jnp.float32)[:, None] * inv_freq[None, :]
    cos = jnp.cos(ang).astype(t.dtype)
    sin = jnp.sin(ang).astype(t.dtype)
    t1 = t[..., :half]
    t2 = t[..., half:ROPE_DIM]
    return jnp.concatenate([t1 * cos - t2 * sin, t2 * cos + t1 * sin, t[..., ROPE_DIM:]], axis=-1)


def gla_chunk_scan(q, k, v, g):
    B, H, L, K = q.shape
    V = v.shape[-1]
    n_chunks = L // HG_CHUNK

    def to_chunks(t):
        return jnp.moveaxis(t.reshape(B, H, n_chunks, HG_CHUNK, t.shape[-1]), 2, 0)

    causal = jnp.tril(jnp.ones((HG_CHUNK, HG_CHUNK), dtype=bool))[:, :, None]

    def step(S, inp):
        qc, kc, vc, gc = inp
        b = jnp.cumsum(gc, axis=2)
        o_inter = jnp.einsum('bhck,bhkv->bhcv', qc * jnp.exp(b), S)
        diff = b[:, :, :, None, :] - b[:, :, None, :, :]
        decay = jnp.where(causal, jnp.exp(jnp.where(causal, diff, 0.0)), 0.0)
        scores = jnp.einsum('bhtk,bhsk,bhtsk->bhts', qc, kc, decay)
        o = o_inter + jnp.einsum('bhts,bhsv->bhtv', scores, vc)
        b_last = b[:, :, -1, :]
        S = jnp.exp(b_last)[..., None] * S + jnp.einsum(
            'bhck,bhcv->bhkv', kc * jnp.exp(b_last[:, :, None, :] - b), vc)
        return S, o

    S0 = jnp.zeros((B, H, K, V), dtype=jnp.float32)
    _, o = lax.scan(step, S0, (to_chunks(q), to_chunks(k), to_chunks(v), to_chunks(g)))
    return jnp.moveaxis(o, 0, 2).reshape(B, H, L, V)


def hgrn2_bidirectional(q_lin, f_fwd, f_bwd, i_lin, lb):
    B, L, _ = q_lin.shape

    def heads(t, d):
        return t.reshape(B, L, HG_HEADS, d).transpose(0, 2, 1, 3).astype(jnp.float32)

    q = heads(jax.nn.silu(q_lin.astype(jnp.float32)), HG_DK)
    v = heads(i_lin, HG_DV)

    def log_forget(z, lb_dir):
        f = lb_dir + (1.0 - lb_dir) * jax.nn.sigmoid(z.astype(jnp.float32))
        return heads(jnp.log(f), HG_DK)

    g_f = log_forget(f_fwd, lb[0])
    g_b = log_forget(f_bwd, lb[1])
    k_f = -jnp.expm1(g_f)
    k_b = -jnp.expm1(g_b)
    flip = lambda t: jnp.flip(t, axis=2)
    o_f = gla_chunk_scan(q, k_f, v, g_f)
    o_b = flip(gla_chunk_scan(flip(q), flip(k_b), flip(v), flip(g_b)))
    return o_f + o_b


def diff_attention(q_lin, k_lin, v_lin, lam, positions):
    B, L, _ = q_lin.shape
    q = q_lin.reshape(B, L, DA_HEADS, 2, DA_DH).transpose(0, 2, 3, 1, 4)
    k = k_lin.reshape(B, L, DA_HEADS, 2, DA_DH).transpose(0, 2, 3, 1, 4)
    v = v_lin.reshape(B, L, DA_HEADS, 2 * DA_DH).transpose(0, 2, 1, 3)
    q = rotary_partial(q, positions)
    k = rotary_partial(k, positions)
    n_blocks = L // DA_Q_BLOCK
    q_blocks = jnp.moveaxis(q.reshape(B, DA_HEADS, 2, n_blocks, DA_Q_BLOCK, DA_DH), 3, 0)
    scale = DA_DH ** -0.5

    def block(qb):
        s = jnp.einsum('bhmqd,bhmkd->bhmqk', qb, k, preferred_element_type=jnp.float32) * scale
        p = jax.nn.softmax(s, axis=-1)
        a = p[:, :, 0] - lam * p[:, :, 1]
        return jnp.einsum('bhqk,bhkv->bhqv', a.astype(v.dtype), v)

    o = lax.map(block, q_blocks)
    return jnp.moveaxis(o, 0, 2).reshape(B, DA_HEADS, L, 2 * DA_DH)


def hybrid_layer(x, c, layer_idx, lb, w_ada, b_ada, g_pre, g_post, w_in, hg_norm_g,
                 da_lambda, da_subln_g, w_proj_hg, w_proj_da, w_out):
    B, L, _ = x.shape
    positions = jnp.arange(L, dtype=jnp.int32)
    mod = jax.nn.silu(c) @ w_ada + b_ada
    shift, scale, gate = jnp.split(mod, 3, axis=-1)
    h = rms_norm(x, g_pre) * (1.0 + scale[:, None, :]) + shift[:, None, :]
    z = h @ w_in
    split_points = np.cumsum(IN_WIDTHS)[:-1].tolist()
    (hg_q, hg_ff, hg_fb, hg_i, hg_gate, da_q, da_k, da_v, da_gate, mg_hg, mg_da) = jnp.split(z, split_points, axis=-1)

    o_hg = hgrn2_bidirectional(hg_q, hg_ff, hg_fb, hg_i, lb)
    o_hg = rms_norm(o_hg, hg_norm_g).transpose(0, 2, 1, 3).reshape(B, L, HG_V_WIDTH)
    o_hg = o_hg.astype(x.dtype) * jax.nn.silu(hg_gate)

    lambda_init = 0.8 - 0.6 * math.exp(-0.3 * layer_idx)
    lf = da_lambda.astype(jnp.float32)
    lam = jnp.exp(jnp.sum(lf[0] * lf[1])) - jnp.exp(jnp.sum(lf[2] * lf[3])) + lambda_init
    o_da = diff_attention(da_q, da_k, da_v, lam, positions)
    o_da = rms_norm(o_da, da_subln_g, SUBLN_EPS) * (1.0 - lambda_init)
    o_da = o_da.transpose(0, 2, 1, 3).reshape(B, L, DA_V_WIDTH).astype(x.dtype) * jax.nn.silu(da_gate)

    merged = jax.nn.sigmoid(mg_hg) * (o_hg @ w_proj_hg) + jax.nn.sigmoid(mg_da) * (o_da @ w_proj_da)
    out = merged @ w_out
    return x + gate[:, None, :] * rms_norm(out, g_post)


def setup_inputs(seed: int = 0) -> dict:
    key = jax.random.key(seed)
    ks = jax.random.split(key, 16)
    f32 = jnp.float32
    s_d = D_MODEL ** -0.5
    return {
        "x_prompt": jax.random.normal(ks[0], (BATCH, SEQ, D_MODEL), f32),
        "x_sample": jax.random.normal(ks[1], (DEC_BATCH, DEC_SEQ, D_MODEL), f32),
        "c_prompt": jax.random.normal(ks[2], (BATCH, D_MODEL), f32),
        "c_sample": jax.random.normal(ks[3], (DEC_BATCH, D_MODEL), f32),
        "w_ada": jax.random.normal(ks[4], (DEPTH, D_MODEL, 3 * D_MODEL), f32) * s_d,
        "b_ada": jax.random.normal(ks[5], (DEPTH, 3 * D_MODEL), f32) * 0.01,
        "g_pre": 1.0 + 0.05 * jax.random.normal(ks[6], (DEPTH, D_MODEL), f32),
        "g_post": 1.0 + 0.05 * jax.random.normal(ks[7], (DEPTH, D_MODEL), f32),
        "w_in": jax.random.normal(ks[8], (DEPTH, D_MODEL, D_IN), f32) * s_d,
        "hg_lower_bounds": 0.1 * jax.random.normal(ks[9], (DEPTH, 2, HG_QK_WIDTH), f32),
        "hg_norm_g": 1.0 + 0.05 * jax.random.normal(ks[10], (DEPTH, HG_DV), f32),
        "da_lambda": 0.1 * jax.random.normal(ks[11], (DEPTH, 4, DA_DH), f32),
        "da_subln_g": 1.0 + 0.05 * jax.random.normal(ks[12], (DEPTH, 2 * DA_DH), f32),
        "w_proj_hg": jax.random.normal(ks[13], (DEPTH, HG_V_WIDTH, D_MODEL), f32) * HG_V_WIDTH ** -0.5,
        "w_proj_da": jax.random.normal(ks[14], (DEPTH, DA_V_WIDTH, D_MODEL), f32) * DA_V_WIDTH ** -0.5,
        "w_out": jax.random.normal(ks[15], (DEPTH, D_MODEL, D_MODEL), f32) * s_d,
    }


def reference(x_prompt, x_sample, c_prompt, c_sample, w_ada, b_ada, g_pre, g_post, w_in,
              hg_lower_bounds, hg_norm_g, da_lambda, da_subln_g, w_proj_hg, w_proj_da, w_out):
    lb_soft = jax.nn.softmax(hg_lower_bounds.astype(jnp.float32), axis=0)
    lb_all = jnp.cumsum(lb_soft, axis=0) - lb_soft[0]
    y_prompt = x_prompt
    y_sample = x_sample
    for l in range(DEPTH):
        y_prompt = hybrid_layer(y_prompt, c_prompt, l, lb_all[l], w_ada[l], b_ada[l], g_pre[l], g_post[l],
                                w_in[l], hg_norm_g[l], da_lambda[l], da_subln_g[l], w_proj_hg[l],
                                w_proj_da[l], w_out[l])
        y_sample = hybrid_layer(y_sample, c_sample, l, lb_all[l], w_ada[l], b_ada[l], g_pre[l], g_post[l],
                                w_in[l], hg_norm_g[l], da_lambda[l], da_subln_g[l], w_proj_hg[l],
                                w_proj_da[l], w_out[l])
    return (y_prompt, y_sample)
```

```python
import functools
import math

import jax
import jax.numpy as jnp
from jax import lax
from jax.experimental import pallas as pl
from jax.experimental.pallas import tpu as pltpu

F32 = jnp.float32
BF16 = jnp.bfloat16

D_MODEL = 1024
HG_HEADS = 4
HG_DK = 128
HG_DV = 128
DA_HEADS = 4
DA_DH = 64
ROPE_THETA = 500000.0
ROPE_DIM = DA_DH // 4
NORM_EPS = 1e-6
SUBLN_EPS = 1e-5
SEG = 512
N_SEG = 13
D_IN = SEG * N_SEG

LANES = 128
HG_CHUNK = 64
HG_SUB = 16
VMEM_LIMIT = 56 * 1024 * 1024


def _dot(a, b):
    return jnp.dot(a, b, preferred_element_type=F32)


def _dot_nt(a, b):
    return lax.dot_general(a, b, (((1,), (1,)), ((), ())), preferred_element_type=F32)


def _dot_tn(a, b):
    return lax.dot_general(a, b, (((0,), (0,)), ((), ())), preferred_element_type=F32)


def _silu(x):
    return x * jax.nn.sigmoid(x)


def _ada_kernel(c_ref, w_ref, b_ref, o_ref):
    c = c_ref[...]
    o_ref[0] = jnp.dot(_silu(c), w_ref[0], precision=lax.Precision.HIGHEST,
                       preferred_element_type=F32) + b_ref[0]


def _ada_mod(c_all, w_ada, b_ada):
    depth = w_ada.shape[0]
    nb = c_all.shape[0]
    tn = D_MODEL
    return pl.pallas_call(
        _ada_kernel,
        out_shape=jax.ShapeDtypeStruct((depth, nb, 3 * D_MODEL), F32),
        grid=(depth, 3 * D_MODEL // tn),
        in_specs=[
            pl.BlockSpec((nb, D_MODEL), lambda l, j: (0, 0)),
            pl.BlockSpec((1, D_MODEL, tn), lambda l, j: (l, 0, j)),
            pl.BlockSpec((1, 1, tn), lambda l, j: (l, 0, j)),
        ],
        out_specs=pl.BlockSpec((1, nb, tn), lambda l, j: (l, 0, j)),
        name="ada_mod",
    )(c_all, w_ada, b_ada.reshape(depth, 1, 3 * D_MODEL))


def _rope_tables(seq_len):
    half = ROPE_DIM // 2
    inv_freq = jnp.power(ROPE_THETA, -jnp.arange(0, ROPE_DIM, 2, dtype=F32) / ROPE_DIM)
    pos = jnp.arange(seq_len, dtype=jnp.int32).astype(F32)
    ang = pos[:, None] * inv_freq[None, :]
    cos = jnp.cos(ang)
    sin = jnp.sin(ang)
    ones = jnp.ones((seq_len, DA_DH - ROPE_DIM), F32)
    zeros = jnp.zeros((seq_len, DA_DH - ROPE_DIM), F32)
    zh = jnp.zeros((seq_len, half), F32)
    cos64 = jnp.concatenate([cos, cos, ones], axis=1)
    sin_up = jnp.concatenate([-sin, zh, zeros], axis=1)
    sin_dn = jnp.concatenate([zh, sin, zeros], axis=1)
    tile2 = lambda t: jnp.concatenate([t, t], axis=1)
    return tile2(cos64), tile2(sin_up), tile2(sin_dn)


def _inproj_kernel(x_ref, mod_ref, gpre_ref, w_ref, lbraw_ref, cos_ref, sup_ref, sdn_ref,
                   hq_ref, gf_ref, gb_ref, hv_ref, hgate_ref, dq_ref, dk_ref, dv_ref, dgate_ref,
                   mgh_ref, mgd_ref, *, layer):
    x = x_ref[...]
    ms = jnp.mean(x * x, axis=-1, keepdims=True)
    y = x * lax.rsqrt(ms + NORM_EPS) * gpre_ref[...]
    shift = mod_ref[0, 0:1, :]
    scale = mod_ref[0, 1:2, :]
    h = (y * (1.0 + scale) + shift).astype(BF16)

    def proj(c0, width=SEG):
        return _dot(h, w_ref[:, c0:c0 + width])

    raw = lbraw_ref[...]
    e = jnp.exp(raw - jnp.max(raw, axis=0, keepdims=True))
    soft = e / jnp.sum(e, axis=0, keepdims=True)
    lb = jnp.sum(soft[0:layer + 1], axis=0) - soft[0]

    def log_forget(z, lb_dir):
        return jnp.log(lb_dir + (1.0 - lb_dir) * jax.nn.sigmoid(z))

    cos = cos_ref[...]
    sup = sup_ref[...]
    sdn = sdn_ref[...]
    half = ROPE_DIM // 2

    def rope(z, out_scale):
        outs = []
        for c in range(SEG // LANES):
            t = z[:, c * LANES:(c + 1) * LANES]
            r = t * cos + pltpu.roll(t, LANES - half, 1) * sup + pltpu.roll(t, half, 1) * sdn
            outs.append(r * out_scale)
        return jnp.concatenate(outs, axis=1)

    hq_ref[...] = _silu(proj(0 * SEG)).astype(BF16)
    gf_ref[...] = log_forget(proj(1 * SEG), lb[0:1])
    gb_ref[...] = log_forget(proj(2 * SEG), lb[1:2])
    hv_ref[...] = proj(3 * SEG).astype(BF16)
    hgate_ref[...] = _silu(proj(4 * SEG)).astype(BF16)
    dq_ref[...] = rope(proj(5 * SEG), DA_DH ** -0.5).astype(BF16)
    dk_ref[...] = rope(proj(6 * SEG), 1.0).astype(BF16)
    dv_ref[...] = proj(7 * SEG).astype(BF16)
    dgate_ref[...] = _silu(proj(8 * SEG)).astype(BF16)
    for c in range(D_MODEL // SEG):
        mgh_ref[:, c * SEG:(c + 1) * SEG] = jax.nn.sigmoid(proj(9 * SEG + c * SEG)).astype(BF16)
        mgd_ref[:, c * SEG:(c + 1) * SEG] = jax.nn.sigmoid(proj(11 * SEG + c * SEG)).astype(BF16)


def _in_proj(x2, mod, g_pre, w_in_bf16, lb_raw, rope_tabs, *, layer, seq_len, tm):
    rows = x2.shape[0]
    nt = seq_len // tm
    row = lambda i: (i, 0)
    const2 = lambda i: (0, 0)
    tab = pl.BlockSpec((tm, LANES), lambda i: (i % nt, 0))
    seg_bf16 = jax.ShapeDtypeStruct((rows, SEG), BF16)
    seg_f32 = jax.ShapeDtypeStruct((rows, SEG), F32)
    wide_bf16 = jax.ShapeDtypeStruct((rows, D_MODEL), BF16)
    seg_spec = pl.BlockSpec((tm, SEG), row)
    wide_spec = pl.BlockSpec((tm, D_MODEL), row)
    return pl.pallas_call(
        functools.partial(_inproj_kernel, layer=layer),
        out_shape=(seg_bf16, seg_f32, seg_f32, seg_bf16, seg_bf16, seg_bf16, seg_bf16, seg_bf16,
                   seg_bf16, wide_bf16, wide_bf16),
        grid=(rows // tm,),
        in_specs=[
            pl.BlockSpec((tm, D_MODEL), row),
            pl.BlockSpec((1, 3, D_MODEL), lambda i: (i // nt, 0, 0)),
            pl.BlockSpec((1, D_MODEL), const2),
            pl.BlockSpec((D_MODEL, D_IN), const2, pipeline_mode=pl.Buffered(1)),
            pl.BlockSpec(lb_raw.shape, lambda i: (0, 0, 0)),
            tab, tab, tab,
        ],
        out_specs=(seg_spec,) * 9 + (wide_spec, wide_spec),
        compiler_params=pltpu.CompilerParams(dimension_semantics=("parallel",),
                                             vmem_limit_bytes=VMEM_LIMIT),
        name="in_proj",
    )(x2, mod, g_pre.reshape(1, D_MODEL), w_in_bf16, lb_raw, *rope_tabs)


def _cumsum_rows(tri_bf16, g):
    hi = g.astype(BF16)
    r1 = g - hi.astype(F32)
    mid = r1.astype(BF16)
    lo = (r1 - mid.astype(F32)).astype(BF16)
    return _dot(tri_bf16, hi) + _dot(tri_bf16, mid) + _dot(tri_bf16, lo)


def _hgrn_chunk(q_ref, g_ref, v_ref, o_ref, st_ref, r0, fwd):
    C, SUB = HG_CHUNK, HG_SUB
    rows = pl.ds(r0, C)
    q = q_ref[0, rows, :].astype(F32)
    g = g_ref[0, rows, :]
    v = v_ref[0, rows, :]
    vf = v.astype(F32)
    k = 1.0 - jnp.exp(g)

    ti = lax.broadcasted_iota(jnp.int32, (C, C), 0)
    si = lax.broadcasted_iota(jnp.int32, (C, C), 1)
    tri = (si <= ti) if fwd else (si >= ti)
    a = _cumsum_rows(tri.astype(BF16), g)
    a_end = a[C - 1:C] if fwd else a[0:1]

    st = st_ref[...]
    o_inter = _dot_nt((q * jnp.exp(a)).astype(BF16), st.astype(BF16))
    kd = (k * jnp.exp(a_end - a)).astype(BF16)
    st_ref[...] = st * jnp.exp(a_end) + _dot_tn(v, kd)

    ones = jnp.ones((LANES, LANES), BF16)
    trow = lax.broadcasted_iota(jnp.int32, (SUB, LANES), 0)
    nsub = C // SUB
    for i in range(nsub):
        lo, hi = i * SUB, (i + 1) * SUB
        ai, qi, ki, vi = a[lo:hi], q[lo:hi], k[lo:hi], vf[lo:hi]
        o_i = o_inter[lo:hi]
        if fwd and i > 0:
            ref, keys = a[lo - 1:lo], slice(0, lo)
        elif (not fwd) and i < nsub - 1:
            ref, keys = a[hi:hi + 1], slice(hi, C)
        else:
            ref, keys = None, None
        if ref is not None:
            qt = (qi * jnp.exp(ai - ref)).astype(BF16)
            kt = (k[keys] * jnp.exp(ref - a[keys])).astype(BF16)
            scores = _dot_nt(qt, kt)
            o_i = o_i + _dot(scores.astype(BF16), v[keys])
        ws = []
        for s in range(SUB):
            valid = (trow >= s) if fwd else (trow <= s)
            diff = jnp.where(valid, ai - ai[s:s + 1], -1e30)
            ws.append((qi * jnp.exp(diff) * ki[s:s + 1]).astype(BF16))
        rsum = _dot(jnp.concatenate(ws, axis=0), ones)
        for s in range(SUB):
            o_i = o_i + rsum[s * SUB:(s + 1) * SUB] * vi[s:s + 1]
        o_ref[0, pl.ds(r0 + lo, SUB), :] = o_i


def _hgrn_kernel(qf_ref, gf_ref, vf_ref, qb_ref, gb_ref, vb_ref, of_ref, ob_ref, sf_ref, sb_ref,
                 *, block_rows):
    @pl.when(pl.program_id(2) == 0)
    def _():
        sf_ref[...] = jnp.zeros_like(sf_ref)
        sb_ref[...] = jnp.zeros_like(sb_ref)

    nch = block_rows // HG_CHUNK

    def body(ci, carry):
        _hgrn_chunk(qf_ref, gf_ref, vf_ref, of_ref, sf_ref, pl.multiple_of(ci * HG_CHUNK, HG_CHUNK), True)
        _hgrn_chunk(qb_ref, gb_ref, vb_ref, ob_ref, sb_ref,
                    pl.multiple_of((nch - 1 - ci) * HG_CHUNK, HG_CHUNK), False)
        return carry

    lax.fori_loop(0, nch, body, 0)


def _hgrn(hq, gf, gb, hv, *, block_rows):
    batch, seq_len, _ = hq.shape
    nblk = seq_len // block_rows
    fwd = pl.BlockSpec((1, block_rows, LANES), lambda b, h, j: (b, j, h))
    bwd = pl.BlockSpec((1, block_rows, LANES), lambda b, h, j: (b, nblk - 1 - j, h))
    out = jax.ShapeDtypeStruct((batch, seq_len, HG_HEADS * HG_DV), F32)
    return pl.pallas_call(
        functools.partial(_hgrn_kernel, block_rows=block_rows),
        out_shape=(out, out),
        grid=(batch, HG_HEADS, nblk),
        in_specs=[fwd, fwd, fwd, bwd, bwd, bwd],
        out_specs=(fwd, bwd),
        scratch_shapes=[pltpu.VMEM((HG_DV, HG_DK), F32), pltpu.VMEM((HG_DV, HG_DK), F32)],
        compiler_params=pltpu.CompilerParams(
            dimension_semantics=("parallel", "parallel", "arbitrary"),
            vmem_limit_bytes=VMEM_LIMIT),
        name="hgrn2",
    )(hq, gf, hv, hq, gb, hv)


def _attn_kernel(q_ref, k_ref, v_ref, gate_ref, lam_ref, sg_ref, o_ref,
                 qbd_ref, m_ref, l_ref, acc_ref, *, qb, lambda_init):
    kv = pl.program_id(3)

    @pl.when(kv == 0)
    def _():
        q = q_ref[0]
        lane = lax.broadcasted_iota(jnp.int32, q.shape, 1)
        zero = jnp.zeros_like(q)
        qbd_ref[0:qb, :] = jnp.where(lane < DA_DH, q, zero)
        qbd_ref[qb:2 * qb, :] = jnp.where(lane >= DA_DH, q, zero)
        m_ref[...] = jnp.full_like(m_ref, -jnp.inf)
        l_ref[...] = jnp.zeros_like(l_ref)
        acc_ref[...] = jnp.zeros_like(acc_ref)

    s = _dot_nt(qbd_ref[...], k_ref[0])
    m_prev = m_ref[...]
    m_new = jnp.maximum(m_prev, jnp.max(s, axis=1, keepdims=True))
    alpha = jnp.exp(m_prev - m_new)
    p = jnp.exp(s - m_new[:, 0:1])
    l_ref[...] = alpha * l_ref[...] + jnp.sum(p, axis=1, keepdims=True)
    acc_ref[...] = alpha * acc_ref[...] + _dot(p.astype(BF16), v_ref[0])
    m_ref[...] = m_new

    @pl.when(kv == pl.num_programs(3) - 1)
    def _():
        lf = lam_ref[...]
        s01 = jnp.sum(lf[0:1] * lf[1:2], axis=1, keepdims=True)
        s23 = jnp.sum(lf[2:3] * lf[3:4], axis=1, keepdims=True)
        lam = jnp.exp(s01) - jnp.exp(s23) + lambda_init
        on = acc_ref[...] / l_ref[...]
        o = on[0:qb] - lam * on[qb:2 * qb]
        ms = jnp.mean(o * o, axis=-1, keepdims=True)
        y = o * lax.rsqrt(ms + SUBLN_EPS) * sg_ref[...] * (1.0 - lambda_init)
        o_ref[0] = (y.astype(F32) * gate_ref[0].astype(F32)).astype(BF16)


def _diff_attn(dq, dk, dv, dgate, da_lambda, da_subln_g, *, layer, qb, kb):
    batch, seq_len, _ = dq.shape
    lambda_init = 0.8 - 0.6 * math.exp(-0.3 * layer)
    qspec = pl.BlockSpec((1, qb, LANES), lambda b, h, i, j: (b, i, h))
    kspec = pl.BlockSpec((1, kb, LANES), lambda b, h, i, j: (b, j, h))
    return pl.pallas_call(
        functools.partial(_attn_kernel, qb=qb, lambda_init=lambda_init),
        out_shape=jax.ShapeDtypeStruct((batch, seq_len, DA_HEADS * 2 * DA_DH), BF16),
        grid=(batch, DA_HEADS, seq_len // qb, seq_len // kb),
        in_specs=[
            qspec, kspec, kspec, qspec,
            pl.BlockSpec((4, DA_DH), lambda b, h, i, j: (0, 0)),
            pl.BlockSpec((1, 2 * DA_DH), lambda b, h, i, j: (0, 0)),
        ],
        out_specs=qspec,
        scratch_shapes=[
            pltpu.VMEM((2 * qb, LANES), BF16),
            pltpu.VMEM((2 * qb, LANES), F32),
            pltpu.VMEM((2 * qb, LANES), F32),
            pltpu.VMEM((2 * qb, LANES), F32),
        ],
        compiler_params=pltpu.CompilerParams(
            dimension_semantics=("parallel", "parallel", "parallel", "arbitrary"),
            vmem_limit_bytes=VMEM_LIMIT),
        name="diff_attn",
    )(dq, dk, dv, dgate, da_lambda, da_subln_g.reshape(1, 2 * DA_DH))


def _out_kernel(x_ref, mod_ref, of_ref, ob_ref, hgate_ref, oda_ref, mgh_ref, mgd_ref,
                hg_g_ref, gpost_ref, wph_ref, wpd_ref, wo_ref, y_ref):
    o = of_ref[...] + ob_ref[...]
    g = hg_g_ref[...]
    heads = []
    for h in range(HG_HEADS):
        oh = o[:, h * HG_DV:(h + 1) * HG_DV]
        ms = jnp.mean(oh * oh, axis=-1, keepdims=True)
        heads.append(oh * lax.rsqrt(ms + NORM_EPS) * g)
    o_hg = (jnp.concatenate(heads, axis=1) * hgate_ref[...].astype(F32)).astype(BF16)
    merged = (mgh_ref[...].astype(F32) * _dot(o_hg, wph_ref[...])
              + mgd_ref[...].astype(F32) * _dot(oda_ref[...], wpd_ref[...]))
    out = _dot(merged.astype(BF16), wo_ref[...])
    ms = jnp.mean(out * out, axis=-1, keepdims=True)
    normed = out * lax.rsqrt(ms + NORM_EPS) * gpost_ref[...]
    y_ref[...] = x_ref[...] + mod_ref[0, 2:3, :] * normed


def _out_stage(x2, mod, of, ob, hgate, oda, mgh, mgd, hg_norm_g, g_post, wph, wpd, wo,
               *, seq_len, tm):
    rows = x2.shape[0]
    nt = seq_len // tm
    row = lambda i: (i, 0)
    const2 = lambda i: (0, 0)
    seg = pl.BlockSpec((tm, SEG), row)
    wide = pl.BlockSpec((tm, D_MODEL), row)
    return pl.pallas_call(
        _out_kernel,
        out_shape=jax.ShapeDtypeStruct((rows, D_MODEL), F32),
        grid=(rows // tm,),
        in_specs=[
            wide,
            pl.BlockSpec((1, 3, D_MODEL), lambda i: (i // nt, 0, 0)),
            seg, seg, seg, seg, wide, wide,
            pl.BlockSpec((1, HG_DV), const2),
            pl.BlockSpec((1, D_MODEL), const2),
            pl.BlockSpec((SEG, D_MODEL), const2),
            pl.BlockSpec((SEG, D_MODEL), const2),
            pl.BlockSpec((D_MODEL, D_MODEL), const2),
        ],
        out_specs=wide,
        compiler_params=pltpu.CompilerParams(dimension_semantics=("parallel",),
                                             vmem_limit_bytes=VMEM_LIMIT),
        name="out_stage",
    )(x2, mod, of, ob, hgate, oda, mgh, mgd, hg_norm_g.reshape(1, HG_DV),
      g_post.reshape(1, D_MODEL), wph, wpd, wo)


def _tiles(seq_len):
    tm = min(256, seq_len)
    hg_rows = min(256, seq_len)
    qb = min(512, seq_len)
    kb = min(512, seq_len)
    return tm, hg_rows, qb, kb


def _layer(x, mod, layer, w_in_bf16, lb_raw, g_pre, g_post, hg_norm_g, da_lambda, da_subln_g,
           wph, wpd, wo, rope_tabs):
    batch, seq_len, _ = x.shape
    tm, hg_rows, qb, kb = _tiles(seq_len)
    rows = batch * seq_len
    x2 = x.reshape(rows, D_MODEL)
    (hq, gf, gb, hv, hgate, dq, dk, dv, dgate, mgh, mgd) = _in_proj(
        x2, mod, g_pre, w_in_bf16, lb_raw, rope_tabs, layer=layer, seq_len=seq_len, tm=tm)
    r3 = lambda t: t.reshape(batch, seq_len, SEG)
    of, ob = _hgrn(r3(hq), r3(gf), r3(gb), r3(hv), block_rows=hg_rows)
    oda = _diff_attn(r3(dq), r3(dk), r3(dv), r3(dgate), da_lambda, da_subln_g,
                     layer=layer, qb=qb, kb=kb)
    y2 = _out_stage(x2, mod, of.reshape(rows, SEG), ob.reshape(rows, SEG), hgate,
                    oda.reshape(rows, SEG), mgh, mgd, hg_norm_g, g_post, wph, wpd, wo,
                    seq_len=seq_len, tm=tm)
    return y2.reshape(batch, seq_len, D_MODEL)


def kernel(x_prompt, x_sample, c_prompt, c_sample, w_ada, b_ada, g_pre, g_post, w_in, hg_lower_bounds, hg_norm_g, da_lambda, da_subln_g, w_proj_hg, w_proj_da, w_out):
    depth = w_in.shape[0]
    nb_prompt = x_prompt.shape[0]
    c_all = jnp.concatenate([c_prompt, c_sample], axis=0)
    mod_all = _ada_mod(c_all, w_ada, b_ada)
    lb_raw = hg_lower_bounds.astype(F32)
    w_in_bf16 = w_in.astype(BF16)
    wph_all = w_proj_hg.astype(BF16)
    wpd_all = w_proj_da.astype(BF16)
    wo_all = w_out.astype(BF16)
    tabs = {x.shape[1]: _rope_tables(x.shape[1]) for x in (x_prompt, x_sample)}

    ys = [x_prompt, x_sample]
    for l in range(depth):
        mod_l = mod_all[l].reshape(-1, 3, D_MODEL)
        mods = (mod_l[:nb_prompt], mod_l[nb_prompt:])
        for gi in range(2):
            x = ys[gi]
            ys[gi] = _layer(x, mods[gi], l, w_in_bf16[l], lb_raw, g_pre[l], g_post[l], hg_norm_g[l],
                            da_lambda[l], da_subln_g[l], wph_all[l], wpd_all[l], wo_all[l],
                            tabs[x.shape[1]])
    return (ys[0], ys[1])
```

```python
import functools
import math

import jax
import jax.numpy as jnp
from jax import lax
from jax.experimental import pallas as pl
from jax.experimental.pallas import tpu as pltpu

F32 = jnp.float32
BF16 = jnp.bfloat16

D_MODEL = 1024
HG_HEADS = 4
HG_DK = 128
HG_DV = 128
DA_HEADS = 4
DA_DH = 64
ROPE_THETA = 500000.0
ROPE_DIM = DA_DH // 4
NORM_EPS = 1e-6
SUBLN_EPS = 1e-5
SEG = 512
N_SEG = 13
D_IN = SEG * N_SEG
LOG2_E = math.log2(math.e)
Q_SCALE = DA_DH ** -0.5 * LOG2_E

LANES = 128
HG_CHUNK = 64
HG_SUB = 16
VMEM_LIMIT = 56 * 1024 * 1024


def _dot(a, b):
    return jnp.dot(a, b, preferred_element_type=F32)


def _dot_nt(a, b):
    return lax.dot_general(a, b, (((1,), (1,)), ((), ())), preferred_element_type=F32)


def _dot_tn(a, b):
    return lax.dot_general(a, b, (((0,), (0,)), ((), ())), preferred_element_type=F32)


def _silu(x):
    return x * jax.nn.sigmoid(x)


def _ada_kernel(c_ref, w_ref, b_ref, o_ref):
    c = c_ref[...]
    o_ref[0] = jnp.dot(_silu(c), w_ref[0], precision=lax.Precision.HIGHEST,
                       preferred_element_type=F32) + b_ref[0]


def _ada_mod(c_all, w_ada, b_ada):
    depth = w_ada.shape[0]
    nb = c_all.shape[0]
    tn = D_MODEL
    return pl.pallas_call(
        _ada_kernel,
        out_shape=jax.ShapeDtypeStruct((depth, nb, 3 * D_MODEL), F32),
        grid=(depth, 3 * D_MODEL // tn),
        in_specs=[
            pl.BlockSpec((nb, D_MODEL), lambda l, j: (0, 0)),
            pl.BlockSpec((1, D_MODEL, tn), lambda l, j: (l, 0, j)),
            pl.BlockSpec((1, 1, tn), lambda l, j: (l, 0, j)),
        ],
        out_specs=pl.BlockSpec((1, nb, tn), lambda l, j: (l, 0, j)),
        name="ada_mod",
    )(c_all, w_ada, b_ada.reshape(depth, 1, 3 * D_MODEL))


def _rope_tables(seq_len):
    half = ROPE_DIM // 2
    inv_freq = jnp.power(ROPE_THETA, -jnp.arange(0, ROPE_DIM, 2, dtype=F32) / ROPE_DIM)
    pos = jnp.arange(seq_len, dtype=jnp.int32).astype(F32)
    ang = pos[:, None] * inv_freq[None, :]
    cos = jnp.cos(ang)
    sin = jnp.sin(ang)
    ones = jnp.ones((seq_len, DA_DH - ROPE_DIM), F32)
    zeros = jnp.zeros((seq_len, DA_DH - ROPE_DIM), F32)
    zh = jnp.zeros((seq_len, half), F32)
    cos64 = jnp.concatenate([cos, cos, ones], axis=1)
    sin_up = jnp.concatenate([-sin, zh, zeros], axis=1)
    sin_dn = jnp.concatenate([zh, sin, zeros], axis=1)
    tile2 = lambda t: jnp.concatenate([t, t], axis=1)
    return tile2(cos64), tile2(sin_up), tile2(sin_dn)


def _inproj_kernel(x_ref, mod_ref, gpre_ref, w_ref, lbraw_ref, cos_ref, sup_ref, sdn_ref,
                   hq_ref, gf_ref, gb_ref, hv_ref, hgate_ref, dq_ref, dk_ref, dv_ref, dgate_ref,
                   mgh_ref, mgd_ref, *, layer):
    x = x_ref[...]
    ms = jnp.mean(x * x, axis=-1, keepdims=True)
    y = x * lax.rsqrt(ms + NORM_EPS) * gpre_ref[...]
    shift = mod_ref[0, 0:1, :]
    scale = mod_ref[0, 1:2, :]
    h = (y * (1.0 + scale) + shift).astype(BF16)

    def proj(c0, width=SEG):
        return _dot(h, w_ref[:, c0:c0 + width])

    raw = lbraw_ref[...]
    e = jnp.exp(raw - jnp.max(raw, axis=0, keepdims=True))
    soft = e / jnp.sum(e, axis=0, keepdims=True)
    lb = jnp.sum(soft[0:layer + 1], axis=0) - soft[0]

    def log2_forget(z, lb_dir):
        return jnp.log(lb_dir + (1.0 - lb_dir) * jax.nn.sigmoid(z)) * LOG2_E

    cos = cos_ref[...]
    sup = sup_ref[...]
    sdn = sdn_ref[...]
    half = ROPE_DIM // 2

    def rope(z, out_scale):
        outs = []
        for c in range(SEG // LANES):
            t = z[:, c * LANES:(c + 1) * LANES]
            r = t * cos + pltpu.roll(t, LANES - half, 1) * sup + pltpu.roll(t, half, 1) * sdn
            outs.append(r * out_scale)
        return jnp.concatenate(outs, axis=1)

    hq_ref[...] = _silu(proj(0 * SEG)).astype(BF16)
    gf_ref[...] = log2_forget(proj(1 * SEG), lb[0:1])
    gb_ref[...] = log2_forget(proj(2 * SEG), lb[1:2])
    hv_ref[...] = proj(3 * SEG).astype(BF16)
    hgate_ref[...] = _silu(proj(4 * SEG)).astype(BF16)
    dq_ref[...] = rope(proj(5 * SEG), Q_SCALE).astype(BF16)
    dk_ref[...] = rope(proj(6 * SEG), 1.0).astype(BF16)
    dv_ref[...] = proj(7 * SEG).astype(BF16)
    dgate_ref[...] = _silu(proj(8 * SEG)).astype(BF16)
    for c in range(D_MODEL // SEG):
        mgh_ref[:, c * SEG:(c + 1) * SEG] = jax.nn.sigmoid(proj(9 * SEG + c * SEG)).astype(BF16)
        mgd_ref[:, c * SEG:(c + 1) * SEG] = jax.nn.sigmoid(proj(11 * SEG + c * SEG)).astype(BF16)


def _in_proj(x2, mod, g_pre, w_in_bf16, lb_raw, rope_tabs, *, layer, seq_len, tm):
    rows = x2.shape[0]
    nt = seq_len // tm
    row = lambda i: (i, 0)
    const2 = lambda i: (0, 0)
    tab = pl.BlockSpec((tm, LANES), lambda i: (i % nt, 0))
    seg_bf16 = jax.ShapeDtypeStruct((rows, SEG), BF16)
    seg_f32 = jax.ShapeDtypeStruct((rows, SEG), F32)
    wide_bf16 = jax.ShapeDtypeStruct((rows, D_MODEL), BF16)
    seg_spec = pl.BlockSpec((tm, SEG), row)
    wide_spec = pl.BlockSpec((tm, D_MODEL), row)
    return pl.pallas_call(
        functools.partial(_inproj_kernel, layer=layer),
        out_shape=(seg_bf16, seg_f32, seg_f32, seg_bf16, seg_bf16, seg_bf16, seg_bf16, seg_bf16,
                   seg_bf16, wide_bf16, wide_bf16),
        grid=(rows // tm,),
        in_specs=[
            pl.BlockSpec((tm, D_MODEL), row),
            pl.BlockSpec((1, 3, D_MODEL), lambda i: (i // nt, 0, 0)),
            pl.BlockSpec((1, D_MODEL), const2),
            pl.BlockSpec((D_MODEL, D_IN), const2, pipeline_mode=pl.Buffered(1)),
            pl.BlockSpec(lb_raw.shape, lambda i: (0, 0, 0)),
            tab, tab, tab,
        ],
        out_specs=(seg_spec,) * 9 + (wide_spec, wide_spec),
        compiler_params=pltpu.CompilerParams(dimension_semantics=("parallel",),
                                             vmem_limit_bytes=VMEM_LIMIT),
        name="in_proj",
    )(x2, mod, g_pre.reshape(1, D_MODEL), w_in_bf16, lb_raw, *rope_tabs)


def _cumsum_rows(tri_bf16, g):
    hi = g.astype(BF16)
    r1 = g - hi.astype(F32)
    mid = r1.astype(BF16)
    lo = (r1 - mid.astype(F32)).astype(BF16)
    return _dot(tri_bf16, hi) + _dot(tri_bf16, mid) + _dot(tri_bf16, lo)


def _hgrn_head(q, k, v, a, a_end, st_ref, ak_ref, head, fwd):
    C, SUB = HG_CHUNK, HG_SUB
    HALF = SUB // 2
    cols = slice(head * HG_DK, (head + 1) * HG_DK)
    st = st_ref[head]
    o_inter = _dot_nt((q * jnp.exp2(a)).astype(BF16), st.astype(BF16))
    kd = (k * jnp.exp2(a_end - a)).astype(BF16)
    st_ref[head] = st * jnp.exp2(a_end) + _dot_tn(v, kd)

    lane = lax.broadcasted_iota(jnp.int32, (HALF, LANES), 1)
    lane2 = lax.broadcasted_iota(jnp.int32, (SUB, LANES), 1)
    trow = lax.broadcasted_iota(jnp.int32, (SUB, LANES), 0)
    nsub = C // SUB
    blocks = []
    for i in range(nsub):
        lo, hi = i * SUB, (i + 1) * SUB
        ai, qi = a[lo:hi], q[lo:hi]
        if fwd and i > 0:
            ref = a[lo - 1:lo]
            kt = (k[0:lo] * jnp.exp2(ref - a[0:lo])).astype(BF16)
            kt = jnp.concatenate([kt, jnp.zeros((LANES - lo, LANES), BF16)], axis=0)
        elif (not fwd) and i < nsub - 1:
            ref = a[hi:hi + 1]
            kt = (k[hi:C] * jnp.exp2(ref - a[hi:C])).astype(BF16)
            kt = jnp.concatenate([jnp.zeros((hi, LANES), BF16), kt,
                                  jnp.zeros((LANES - C, LANES), BF16)], axis=0)
        else:
            ref = None
        if ref is not None:
            sc = _dot_nt((qi * jnp.exp2(ai - ref)).astype(BF16), kt)
        else:
            sc = jnp.zeros((SUB, LANES), F32)
        sc_a, sc_b = sc[0:HALF], sc[HALF:SUB]
        qa, qb_, aa, ab = qi[0:HALF], qi[HALF:SUB], ai[0:HALF], ai[HALF:SUB]
        for s in range(SUB):
            a_s = ak_ref[0, head, lo + s:lo + s + 1, :]
            k_s = ak_ref[1, head, lo + s:lo + s + 1, :]
            need_a = (s < HALF) if fwd else True
            need_b = True if fwd else (s >= HALF)
            if need_a:
                col = jnp.sum(qa * jnp.exp2(aa - a_s) * k_s, axis=1, keepdims=True)
                sc_a = jnp.where(lane == lo + s, col, sc_a)
            if need_b:
                col = jnp.sum(qb_ * jnp.exp2(ab - a_s) * k_s, axis=1, keepdims=True)
                sc_b = jnp.where(lane == lo + s, col, sc_b)
        sc = jnp.concatenate([sc_a, sc_b], axis=0)
        seen = (lane2 <= trow + lo) if fwd else (lane2 >= trow + lo)
        blocks.append(jnp.where(seen, sc, 0.0))
    scores = jnp.concatenate(blocks, axis=0)[:, 0:C].astype(BF16)
    return o_inter + _dot(scores, v)


def _hgrn_chunk(q_ref, g_ref, v_ref, o_ref, st_ref, ak_ref, r0, fwd):
    C = HG_CHUNK
    rows = pl.ds(r0, C)
    g = g_ref[0, rows, :]
    ti = lax.broadcasted_iota(jnp.int32, (C, C), 0)
    si = lax.broadcasted_iota(jnp.int32, (C, C), 1)
    tri = (si <= ti) if fwd else (si >= ti)
    a_all = _cumsum_rows(tri.astype(BF16), g)
    k_all = 1.0 - jnp.exp2(g)
    for h in range(HG_HEADS):
        cols = slice(h * HG_DK, (h + 1) * HG_DK)
        a = a_all[:, cols]
        ak_ref[0, h] = a
        ak_ref[1, h] = k_all[:, cols]
        a_end = a[C - 1:C] if fwd else a[0:1]
        q = q_ref[0, rows, cols].astype(F32)
        v = v_ref[0, rows, cols]
        o_ref[0, rows, cols] = _hgrn_head(q, k_all[:, cols], v, a, a_end, st_ref, ak_ref, h, fwd)


def _hgrn_kernel(qf_ref, gf_ref, vf_ref, qb_ref, gb_ref, vb_ref, of_ref, ob_ref, sf_ref, sb_ref,
                 akf_ref, akb_ref, *, block_rows):
    @pl.when(pl.program_id(1) == 0)
    def _():
        sf_ref[...] = jnp.zeros_like(sf_ref)
        sb_ref[...] = jnp.zeros_like(sb_ref)

    nch = block_rows // HG_CHUNK

    def body(ci, carry):
        _hgrn_chunk(qf_ref, gf_ref, vf_ref, of_ref, sf_ref, akf_ref,
                    pl.multiple_of(ci * HG_CHUNK, HG_CHUNK), True)
        _hgrn_chunk(qb_ref, gb_ref, vb_ref, ob_ref, sb_ref, akb_ref,
                    pl.multiple_of((nch - 1 - ci) * HG_CHUNK, HG_CHUNK), False)
        return carry

    lax.fori_loop(0, nch, body, 0)


def _hgrn(hq, gf, gb, hv, *, block_rows):
    batch, seq_len, width = hq.shape
    nblk = seq_len // block_rows
    fwd = pl.BlockSpec((1, block_rows, width), lambda b, j: (b, j, 0))
    bwd = pl.BlockSpec((1, block_rows, width), lambda b, j: (b, nblk - 1 - j, 0))
    out = jax.ShapeDtypeStruct((batch, seq_len, HG_HEADS * HG_DV), F32)
    state = pltpu.VMEM((HG_HEADS, HG_DV, HG_DK), F32)
    staged = pltpu.VMEM((2, HG_HEADS, HG_CHUNK, HG_DK), F32)
    return pl.pallas_call(
        functools.partial(_hgrn_kernel, block_rows=block_rows),
        out_shape=(out, out),
        grid=(batch, nblk),
        in_specs=[fwd, fwd, fwd, bwd, bwd, bwd],
        out_specs=(fwd, bwd),
        scratch_shapes=[state, state, staged, staged],
        compiler_params=pltpu.CompilerParams(
            dimension_semantics=("parallel", "arbitrary"),
            vmem_limit_bytes=VMEM_LIMIT),
        name="hgrn2",
    )(hq, gf, hv, hq, gb, hv)


def _attn_kernel(q_ref, k_ref, v_ref, gate_ref, lam_ref, sg_ref, o_ref,
                 qbd_ref, s0_ref, s1_ref, m_ref, acc_ref, *, qb, kb, lambda_init):
    nk = k_ref.shape[1] // kb
    q = q_ref[0]
    lane = lax.broadcasted_iota(jnp.int32, q.shape, 1)
    zero = jnp.zeros_like(q)
    qbd_ref[0:qb, :] = jnp.where(lane < DA_DH, q, zero)
    qbd_ref[qb:2 * qb, :] = jnp.where(lane >= DA_DH, q, zero)
    m_ref[...] = jnp.full_like(m_ref, -jnp.inf)
    acc_ref[...] = jnp.zeros_like(acc_ref)

    def rows(j):
        return pl.ds(pl.multiple_of(j * kb, kb), kb)

    def scores(s_ref, j):
        s_ref[...] = _dot_nt(qbd_ref[...], k_ref[0, rows(j), :])

    def update(s_ref, j):
        s = s_ref[...]
        m_prev = m_ref[...]
        m_new = jnp.maximum(m_prev, jnp.max(s, axis=1, keepdims=True))
        alpha = jnp.exp2(m_prev - m_new)
        p = jnp.exp2(s - jnp.tile(m_new, (1, kb // LANES)))
        v = v_ref[0, rows(j), :]
        v_ext = jnp.concatenate([v, jnp.ones_like(v)], axis=1)
        acc_ref[...] = jnp.tile(alpha, (1, 2)) * acc_ref[...] + _dot(p.astype(BF16), v_ext)
        m_ref[...] = m_new

    scores(s0_ref, 0)

    def pair(jj, carry):
        j = 2 * jj
        scores(s1_ref, j + 1)
        update(s0_ref, j)
        scores(s0_ref, j + 2)
        update(s1_ref, j + 1)
        return carry

    lax.fori_loop(0, nk // 2 - 1, pair, 0)
    scores(s1_ref, nk - 1)
    update(s0_ref, nk - 2)
    update(s1_ref, nk - 1)

    lf = lam_ref[...]
    s01 = jnp.sum(lf[0:1] * lf[1:2], axis=1, keepdims=True)
    s23 = jnp.sum(lf[2:3] * lf[3:4], axis=1, keepdims=True)
    lam = jnp.exp(s01) - jnp.exp(s23) + lambda_init
    acc = acc_ref[...]
    on = acc[:, 0:LANES] / acc[:, LANES:2 * LANES]
    o = on[0:qb] - lam * on[qb:2 * qb]
    ms = jnp.mean(o * o, axis=-1, keepdims=True)
    y = o * lax.rsqrt(ms + SUBLN_EPS) * sg_ref[...] * (1.0 - lambda_init)
    o_ref[0] = (y * gate_ref[0].astype(F32)).astype(BF16)


def _diff_attn(dq, dk, dv, dgate, da_lambda, da_subln_g, *, layer, qb, kb):
    batch, seq_len, _ = dq.shape
    assert seq_len % (2 * kb) == 0 and seq_len % qb == 0
    lambda_init = 0.8 - 0.6 * math.exp(-0.3 * layer)
    qspec = pl.BlockSpec((1, qb, LANES), lambda b, h, i: (b, i, h))
    kspec = pl.BlockSpec((1, seq_len, LANES), lambda b, h, i: (b, 0, h))
    return pl.pallas_call(
        functools.partial(_attn_kernel, qb=qb, kb=kb, lambda_init=lambda_init),
        out_shape=jax.ShapeDtypeStruct((batch, seq_len, DA_HEADS * 2 * DA_DH), BF16),
        grid=(batch, DA_HEADS, seq_len // qb),
        in_specs=[
            qspec, kspec, kspec, qspec,
            pl.BlockSpec((4, DA_DH), lambda b, h, i: (0, 0)),
            pl.BlockSpec((1, 2 * DA_DH), lambda b, h, i: (0, 0)),
        ],
        out_specs=qspec,
        scratch_shapes=[
            pltpu.VMEM((2 * qb, LANES), BF16),
            pltpu.VMEM((2 * qb, kb), F32),
            pltpu.VMEM((2 * qb, kb), F32),
            pltpu.VMEM((2 * qb, LANES), F32),
            pltpu.VMEM((2 * qb, 2 * LANES), F32),
        ],
        compiler_params=pltpu.CompilerParams(
            dimension_semantics=("parallel", "parallel", "parallel"),
            vmem_limit_bytes=VMEM_LIMIT),
        name="diff_attn",
    )(dq, dk, dv, dgate, da_lambda, da_subln_g.reshape(1, 2 * DA_DH))


def _out_kernel(x_ref, mod_ref, of_ref, ob_ref, hgate_ref, oda_ref, mgh_ref, mgd_ref,
                hg_g_ref, gpost_ref, wph_ref, wpd_ref, wo_ref, y_ref):
    o = of_ref[...] + ob_ref[...]
    g = hg_g_ref[...]
    heads = []
    for h in range(HG_HEADS):
        oh = o[:, h * HG_DV:(h + 1) * HG_DV]
        ms = jnp.mean(oh * oh, axis=-1, keepdims=True)
        heads.append(oh * lax.rsqrt(ms + NORM_EPS) * g)
    o_hg = (jnp.concatenate(heads, axis=1) * hgate_ref[...].astype(F32)).astype(BF16)
    merged = (mgh_ref[...].astype(F32) * _dot(o_hg, wph_ref[...])
              + mgd_ref[...].astype(F32) * _dot(oda_ref[...], wpd_ref[...]))
    out = _dot(merged.astype(BF16), wo_ref[...])
    ms = jnp.mean(out * out, axis=-1, keepdims=True)
    normed = out * lax.rsqrt(ms + NORM_EPS) * gpost_ref[...]
    y_ref[...] = x_ref[...] + mod_ref[0, 2:3, :] * normed


def _out_stage(x2, mod, of, ob, hgate, oda, mgh, mgd, hg_norm_g, g_post, wph, wpd, wo,
               *, seq_len, tm):
    rows = x2.shape[0]
    nt = seq_len // tm
    row = lambda i: (i, 0)
    const2 = lambda i: (0, 0)
    seg = pl.BlockSpec((tm, SEG), row)
    wide = pl.BlockSpec((tm, D_MODEL), row)
    return pl.pallas_call(
        _out_kernel,
        out_shape=jax.ShapeDtypeStruct((rows, D_MODEL), F32),
        grid=(rows // tm,),
        in_specs=[
            wide,
            pl.BlockSpec((1, 3, D_MODEL), lambda i: (i // nt, 0, 0)),
            seg, seg, seg, seg, wide, wide,
            pl.BlockSpec((1, HG_DV), const2),
            pl.BlockSpec((1, D_MODEL), const2),
            pl.BlockSpec((SEG, D_MODEL), const2),
            pl.BlockSpec((SEG, D_MODEL), const2),
            pl.BlockSpec((D_MODEL, D_MODEL), const2),
        ],
        out_specs=wide,
        compiler_params=pltpu.CompilerParams(dimension_semantics=("parallel",),
                                             vmem_limit_bytes=VMEM_LIMIT),
        name="out_stage",
    )(x2, mod, of, ob, hgate, oda, mgh, mgd, hg_norm_g.reshape(1, HG_DV),
      g_post.reshape(1, D_MODEL), wph, wpd, wo)


def _tiles(seq_len):
    tm = min(256, seq_len)
    hg_rows = min(256, seq_len)
    qb = min(512, seq_len)
    kb = min(512, seq_len // 2)
    return tm, hg_rows, qb, kb


def _layer(x, mod, layer, w_in_bf16, lb_raw, g_pre, g_post, hg_norm_g, da_lambda, da_subln_g,
           wph, wpd, wo, rope_tabs):
    batch, seq_len, _ = x.shape
    tm, hg_rows, qb, kb = _tiles(seq_len)
    rows = batch * seq_len
    x2 = x.reshape(rows, D_MODEL)
    (hq, gf, gb, hv, hgate, dq, dk, dv, dgate, mgh, mgd) = _in_proj(
        x2, mod, g_pre, w_in_bf16, lb_raw, rope_tabs, layer=layer, seq_len=seq_len, tm=tm)
    r3 = lambda t: t.reshape(batch, seq_len, SEG)
    of, ob = _hgrn(r3(hq), r3(gf), r3(gb), r3(hv), block_rows=hg_rows)
    oda = _diff_attn(r3(dq), r3(dk), r3(dv), r3(dgate), da_lambda, da_subln_g,
                     layer=layer, qb=qb, kb=kb)
    y2 = _out_stage(x2, mod, of.reshape(rows, SEG), ob.reshape(rows, SEG), hgate,
                    oda.reshape(rows, SEG), mgh, mgd, hg_norm_g, g_post, wph, wpd, wo,
                    seq_len=seq_len, tm=tm)
    return y2.reshape(batch, seq_len, D_MODEL)


def kernel(x_prompt, x_sample, c_prompt, c_sample, w_ada, b_ada, g_pre, g_post, w_in, hg_lower_bounds, hg_norm_g, da_lambda, da_subln_g, w_proj_hg, w_proj_da, w_out):
    depth = w_in.shape[0]
    nb_prompt = x_prompt.shape[0]
    c_all = jnp.concatenate([c_prompt, c_sample], axis=0)
    mod_all = _ada_mod(c_all, w_ada, b_ada)
    lb_raw = hg_lower_bounds.astype(F32)
    w_in_bf16 = w_in.astype(BF16)
    wph_all = w_proj_hg.astype(BF16)
    wpd_all = w_proj_da.astype(BF16)
    wo_all = w_out.astype(BF16)
    tabs = {x.shape[1]: _rope_tables(x.shape[1]) for x in (x_prompt, x_sample)}

    ys = [x_prompt, x_sample]
    for l in range(depth):
        mod_l = mod_all[l].reshape(-1, 3, D_MODEL)
        mods = (mod_l[:nb_prompt], mod_l[nb_prompt:])
        for gi in range(2):
            x = ys[gi]
            ys[gi] = _layer(x, mods[gi], l, w_in_bf16[l], lb_raw, g_pre[l], g_post[l], hg_norm_g[l],
                            da_lambda[l], da_subln_g[l], wph_all[l], wpd_all[l], wo_all[l],
                            tabs[x.shape[1]])
    return (ys[0], ys[1])
```

```python
import functools
import math

import jax
import jax.numpy as jnp
from jax import lax
from jax.experimental import pallas as pl
from jax.experimental.pallas import tpu as pltpu

F32 = jnp.float32
BF16 = jnp.bfloat16

D_MODEL = 1024
HG_HEADS = 4
HG_DK = 128
HG_DV = 128
DA_HEADS = 4
DA_DH = 64
ROPE_THETA = 500000.0
ROPE_DIM = DA_DH // 4
NORM_EPS = 1e-6
SUBLN_EPS = 1e-5
SEG = 512
N_SEG = 13
D_IN = SEG * N_SEG
LOG2_E = math.log2(math.e)
Q_SCALE = DA_DH ** -0.5 * LOG2_E

LANES = 128
IN_SUB_ROWS = 256
HG_CHUNK = 64
HG_SUB = 16
HG_MAX_FACTORED_LOG2_DECAY = 96.0
VMEM_LIMIT = 56 * 1024 * 1024


def _dot(a, b):
    return jnp.dot(a, b, preferred_element_type=F32)


def _dot_nt(a, b):
    return lax.dot_general(a, b, (((1,), (1,)), ((), ())), preferred_element_type=F32)


def _dot_tn(a, b):
    return lax.dot_general(a, b, (((0,), (0,)), ((), ())), preferred_element_type=F32)


def _silu(x):
    return x * jax.nn.sigmoid(x)


def _ada_kernel(c_ref, w_ref, b_ref, o_ref):
    c = c_ref[...]
    o_ref[0] = jnp.dot(_silu(c), w_ref[0], precision=lax.Precision.HIGHEST,
                       preferred_element_type=F32) + b_ref[0]


def _ada_mod(c_all, w_ada, b_ada):
    depth = w_ada.shape[0]
    nb = c_all.shape[0]
    tn = D_MODEL
    return pl.pallas_call(
        _ada_kernel,
        out_shape=jax.ShapeDtypeStruct((depth, nb, 3 * D_MODEL), F32),
        grid=(depth, 3 * D_MODEL // tn),
        in_specs=[
            pl.BlockSpec((nb, D_MODEL), lambda l, j: (0, 0)),
            pl.BlockSpec((1, D_MODEL, tn), lambda l, j: (l, 0, j)),
            pl.BlockSpec((1, 1, tn), lambda l, j: (l, 0, j)),
        ],
        out_specs=pl.BlockSpec((1, nb, tn), lambda l, j: (l, 0, j)),
        name="ada_mod",
    )(c_all, w_ada, b_ada.reshape(depth, 1, 3 * D_MODEL))


def _rope_tables(seq_len):
    half = ROPE_DIM // 2
    inv_freq = jnp.power(ROPE_THETA, -jnp.arange(0, ROPE_DIM, 2, dtype=F32) / ROPE_DIM)
    pos = jnp.arange(seq_len, dtype=jnp.int32).astype(F32)
    ang = pos[:, None] * inv_freq[None, :]
    cos = jnp.cos(ang)
    sin = jnp.sin(ang)
    ones = jnp.ones((seq_len, DA_DH - ROPE_DIM), F32)
    zeros = jnp.zeros((seq_len, DA_DH - ROPE_DIM), F32)
    zh = jnp.zeros((seq_len, half), F32)
    cos64 = jnp.concatenate([cos, cos, ones], axis=1)
    sin_up = jnp.concatenate([-sin, zh, zeros], axis=1)
    sin_dn = jnp.concatenate([zh, sin, zeros], axis=1)
    tile2 = lambda t: jnp.concatenate([t, t], axis=1)
    return tile2(cos64), tile2(sin_up), tile2(sin_dn)


def _inproj_kernel(x_ref, mod_ref, gpre_ref, w_ref, lbraw_ref, cos_ref, sup_ref, sdn_ref,
                   hq_ref, gf_ref, gb_ref, hv_ref, hgate_ref, dq_ref, dk_ref, dv_ref, dgate_ref,
                   mgh_ref, mgd_ref, *, layer):
    raw = lbraw_ref[...]
    e = jnp.exp(raw - jnp.max(raw, axis=0, keepdims=True))
    soft = e / jnp.sum(e, axis=0, keepdims=True)
    lb = jnp.sum(soft[0:layer + 1], axis=0) - soft[0]
    shift = mod_ref[0, 0:1, :]
    scale = mod_ref[0, 1:2, :]
    half = ROPE_DIM // 2

    def log2_forget(z, lb_dir):
        return jnp.log(lb_dir + (1.0 - lb_dir) * jax.nn.sigmoid(z)) * LOG2_E

    for r in range(x_ref.shape[0] // IN_SUB_ROWS):
        rs = slice(r * IN_SUB_ROWS, (r + 1) * IN_SUB_ROWS)
        x = x_ref[rs, :]
        ms = jnp.mean(x * x, axis=-1, keepdims=True)
        y = x * lax.rsqrt(ms + NORM_EPS) * gpre_ref[...]
        h = (y * (1.0 + scale) + shift).astype(BF16)
        cos = cos_ref[rs, :]
        sup = sup_ref[rs, :]
        sdn = sdn_ref[rs, :]

        def proj(c0, h=h):
            return _dot(h, w_ref[:, c0:c0 + SEG])

        def rope(z, out_scale, cos=cos, sup=sup, sdn=sdn):
            outs = []
            for c in range(SEG // LANES):
                t = z[:, c * LANES:(c + 1) * LANES]
                rot = t * cos + pltpu.roll(t, LANES - half, 1) * sup + pltpu.roll(t, half, 1) * sdn
                outs.append(rot * out_scale)
            return jnp.concatenate(outs, axis=1)

        hq_ref[rs, :] = _silu(proj(0 * SEG)).astype(BF16)
        gf_ref[rs, :] = log2_forget(proj(1 * SEG), lb[0:1])
        gb_ref[rs, :] = log2_forget(proj(2 * SEG), lb[1:2])
        hv_ref[rs, :] = proj(3 * SEG).astype(BF16)
        hgate_ref[rs, :] = _silu(proj(4 * SEG)).astype(BF16)
        dq_ref[rs, :] = rope(proj(5 * SEG), Q_SCALE).astype(BF16)
        dk_ref[rs, :] = rope(proj(6 * SEG), 1.0).astype(BF16)
        dv_ref[rs, :] = proj(7 * SEG).astype(BF16)
        dgate_ref[rs, :] = _silu(proj(8 * SEG)).astype(BF16)
        for c in range(D_MODEL // SEG):
            cs = slice(c * SEG, (c + 1) * SEG)
            mgh_ref[rs, cs] = jax.nn.sigmoid(proj(9 * SEG + c * SEG)).astype(BF16)
            mgd_ref[rs, cs] = jax.nn.sigmoid(proj(11 * SEG + c * SEG)).astype(BF16)


def _in_proj(x2, mod, g_pre, w_in_bf16, lb_raw, rope_tabs, *, layer, seq_len, tm):
    rows = x2.shape[0]
    assert tm % IN_SUB_ROWS == 0 and seq_len % tm == 0
    nt = seq_len // tm
    row = lambda i: (i, 0)
    const2 = lambda i: (0, 0)
    tab = pl.BlockSpec((tm, LANES), lambda i: (i % nt, 0))
    seg_bf16 = jax.ShapeDtypeStruct((rows, SEG), BF16)
    seg_f32 = jax.ShapeDtypeStruct((rows, SEG), F32)
    wide_bf16 = jax.ShapeDtypeStruct((rows, D_MODEL), BF16)
    seg_spec = pl.BlockSpec((tm, SEG), row)
    wide_spec = pl.BlockSpec((tm, D_MODEL), row)
    return pl.pallas_call(
        functools.partial(_inproj_kernel, layer=layer),
        out_shape=(seg_bf16, seg_f32, seg_f32, seg_bf16, seg_bf16, seg_bf16, seg_bf16, seg_bf16,
                   seg_bf16, wide_bf16, wide_bf16),
        grid=(rows // tm,),
        in_specs=[
            pl.BlockSpec((tm, D_MODEL), row),
            pl.BlockSpec((1, 3, D_MODEL), lambda i: (i // nt, 0, 0)),
            pl.BlockSpec((1, D_MODEL), const2),
            pl.BlockSpec((D_MODEL, D_IN), const2, pipeline_mode=pl.Buffered(1)),
            pl.BlockSpec(lb_raw.shape, lambda i: (0, 0, 0)),
            tab, tab, tab,
        ],
        out_specs=(seg_spec,) * 9 + (wide_spec, wide_spec),
        compiler_params=pltpu.CompilerParams(dimension_semantics=("parallel",),
                                             vmem_limit_bytes=VMEM_LIMIT),
        name="in_proj",
    )(x2, mod, g_pre.reshape(1, D_MODEL), w_in_bf16, lb_raw, *rope_tabs)


def _cumsum_rows(tri_bf16, g):
    hi = g.astype(BF16)
    r1 = g - hi.astype(F32)
    mid = r1.astype(BF16)
    lo = (r1 - mid.astype(F32)).astype(BF16)
    return _dot(tri_bf16, hi) + _dot(tri_bf16, mid) + _dot(tri_bf16, lo)


def _hgrn_head_matmuls(q, k, v, a, a_end, st_ref, head, fwd, exact_diag):
    C, SUB = HG_CHUNK, HG_SUB
    st = st_ref[head]
    o_inter = _dot_nt((q * jnp.exp2(a)).astype(BF16), st.astype(BF16))
    kd = (k * jnp.exp2(a_end - a)).astype(BF16)
    st_ref[head] = st * jnp.exp2(a_end) + _dot_tn(v, kd)

    nsub = C // SUB
    blocks = []
    for i in range(nsub):
        lo, hi = i * SUB, (i + 1) * SUB
        ai, qi = a[lo:hi], q[lo:hi]
        if fwd:
            ref = a[lo - 1:lo] if i > 0 else jnp.zeros((1, LANES), F32)
            k_lo, k_hi = 0, (lo if exact_diag else hi)
        else:
            ref = a[hi:hi + 1] if i < nsub - 1 else jnp.zeros((1, LANES), F32)
            k_lo, k_hi = (hi if exact_diag else lo), C
        if k_hi > k_lo:
            kt = (k[k_lo:k_hi] * jnp.exp2(ref - a[k_lo:k_hi])).astype(BF16)
            pads = [jnp.zeros((n, LANES), BF16) for n in (k_lo, LANES - k_hi)]
            kt = jnp.concatenate([p for p in (pads[0], kt, pads[1]) if p.shape[0]], axis=0)
            sc = _dot_nt((qi * jnp.exp2(ai - ref)).astype(BF16), kt)
        else:
            sc = jnp.zeros((SUB, LANES), F32)
        blocks.append(sc)
    return o_inter, blocks


def _hgrn_head_finish(q, v, a, o_inter, blocks, ak_ref, head, r0, fwd, exact_diag):
    C, SUB = HG_CHUNK, HG_SUB
    HALF = SUB // 2
    lane = lax.broadcasted_iota(jnp.int32, (HALF, LANES), 1)
    lane2 = lax.broadcasted_iota(jnp.int32, (SUB, LANES), 1)
    trow = lax.broadcasted_iota(jnp.int32, (SUB, LANES), 0)
    masked = []
    for i, sc in enumerate(blocks):
        lo, hi = i * SUB, (i + 1) * SUB
        ai, qi = a[lo:hi], q[lo:hi]
        sc_a, sc_b = sc[0:HALF], sc[HALF:SUB]
        qa, qb_, aa, ab = qi[0:HALF], qi[HALF:SUB], ai[0:HALF], ai[HALF:SUB]
        for s in range(SUB if exact_diag else 0):
            a_s = ak_ref[0, head, r0 + lo + s:r0 + lo + s + 1, :]
            k_s = ak_ref[1, head, r0 + lo + s:r0 + lo + s + 1, :]
            need_a = (s < HALF) if fwd else True
            need_b = True if fwd else (s >= HALF)
            if need_a:
                col = jnp.sum(qa * jnp.exp2(aa - a_s) * k_s, axis=1, keepdims=True)
                sc_a = jnp.where(lane == lo + s, col, sc_a)
            if need_b:
                col = jnp.sum(qb_ * jnp.exp2(ab - a_s) * k_s, axis=1, keepdims=True)
                sc_b = jnp.where(lane == lo + s, col, sc_b)
        sc = jnp.concatenate([sc_a, sc_b], axis=0)
        seen = (lane2 <= trow + lo) if fwd else (lane2 >= trow + lo)
        masked.append(jnp.where(seen, sc, 0.0))
    scores = jnp.concatenate(masked, axis=0)[:, 0:C].astype(BF16)
    return o_inter + _dot(scores, v)


def _hgrn_block(fwd_refs, bwd_refs, nch, exact_diag):
    C = HG_CHUNK
    T = nch * C
    ti = lax.broadcasted_iota(jnp.int32, (T, T), 0)
    si = lax.broadcasted_iota(jnp.int32, (T, T), 1)
    same_chunk = (ti // C) == (si // C)
    scans = ((fwd_refs, True), (bwd_refs, False))
    for (q_ref, g_ref, v_ref, o_ref, st_ref, ak_ref), fwd in scans:
        g = g_ref[0]
        tri = jnp.logical_and(same_chunk, (si <= ti) if fwd else (si >= ti))
        a_all = _cumsum_rows(tri.astype(BF16), g)
        k_all = 1.0 - jnp.exp2(g)
        for h in range(HG_HEADS):
            cols = slice(h * HG_DK, (h + 1) * HG_DK)
            ak_ref[0, h] = a_all[:, cols]
            ak_ref[1, h] = k_all[:, cols]

    def issue(step):
        pending = []
        for (q_ref, g_ref, v_ref, o_ref, st_ref, ak_ref), fwd in scans:
            r0 = (step if fwd else nch - 1 - step) * C
            rows = slice(r0, r0 + C)
            for h in range(HG_HEADS):
                cols = slice(h * HG_DK, (h + 1) * HG_DK)
                a = ak_ref[0, h, rows, :]
                k = ak_ref[1, h, rows, :]
                a_end = a[C - 1:C] if fwd else a[0:1]
                q = q_ref[0, rows, cols].astype(F32)
                v = v_ref[0, rows, cols]
                o_inter, blocks = _hgrn_head_matmuls(q, k, v, a, a_end, st_ref, h, fwd, exact_diag)
                pending.append((q, v, a, o_inter, blocks, ak_ref, h, fwd, o_ref, r0, cols))
        return pending

    def finish(pending):
        for q, v, a, o_inter, blocks, ak_ref, h, fwd, o_ref, r0, cols in pending:
            o_ref[0, r0:r0 + C, cols] = _hgrn_head_finish(q, v, a, o_inter, blocks, ak_ref, h, r0,
                                                          fwd, exact_diag)

    pending = issue(0)
    for step in range(nch):
        following = issue(step + 1) if step + 1 < nch else []
        finish(pending)
        pending = following


def _min_subblock_log2_decay(g_ref):
    g = g_ref[0]
    rows, width = g.shape
    sums = jnp.sum(g.reshape(rows // HG_SUB, HG_SUB, width), axis=1)
    return jnp.min(sums)


def _hgrn_kernel(qf_ref, gf_ref, vf_ref, qb_ref, gb_ref, vb_ref, of_ref, ob_ref, sf_ref, sb_ref,
                 akf_ref, akb_ref, *, block_rows):
    @pl.when(pl.program_id(1) == 0)
    def _():
        sf_ref[...] = jnp.zeros_like(sf_ref)
        sb_ref[...] = jnp.zeros_like(sb_ref)

    nch = block_rows // HG_CHUNK

    def run(exact_diag):
        _hgrn_block((qf_ref, gf_ref, vf_ref, of_ref, sf_ref, akf_ref),
                    (qb_ref, gb_ref, vb_ref, ob_ref, sb_ref, akb_ref), nch, exact_diag)

    worst = jnp.minimum(_min_subblock_log2_decay(gf_ref), _min_subblock_log2_decay(gb_ref))
    mild = worst >= -HG_MAX_FACTORED_LOG2_DECAY

    @pl.when(mild)
    def _():
        run(False)

    @pl.when(jnp.logical_not(mild))
    def _():
        run(True)


def _hgrn(hq, gf, gb, hv, *, block_rows):
    batch, seq_len, width = hq.shape
    nblk = seq_len // block_rows
    fwd = pl.BlockSpec((1, block_rows, width), lambda b, j: (b, j, 0))
    bwd = pl.BlockSpec((1, block_rows, width), lambda b, j: (b, nblk - 1 - j, 0))
    out = jax.ShapeDtypeStruct((batch, seq_len, HG_HEADS * HG_DV), F32)
    state = pltpu.VMEM((HG_HEADS, HG_DV, HG_DK), F32)
    staged = pltpu.VMEM((2, HG_HEADS, block_rows, HG_DK), F32)
    return pl.pallas_call(
        functools.partial(_hgrn_kernel, block_rows=block_rows),
        out_shape=(out, out),
        grid=(batch, nblk),
        in_specs=[fwd, fwd, fwd, bwd, bwd, bwd],
        out_specs=(fwd, bwd),
        scratch_shapes=[state, state, staged, staged],
        compiler_params=pltpu.CompilerParams(
            dimension_semantics=("parallel", "arbitrary"),
            vmem_limit_bytes=VMEM_LIMIT),
        name="hgrn2",
    )(hq, gf, hv, hq, gb, hv)


def _attn_kernel(q_ref, k_ref, v_ref, gate_ref, lam_ref, sg_ref, o_ref,
                 qbd_ref, s0_ref, s1_ref, m_ref, acc_ref, *, qb, kb, unroll, lambda_init):
    nk = k_ref.shape[1] // kb
    q = q_ref[0]
    lane = lax.broadcasted_iota(jnp.int32, q.shape, 1)
    zero = jnp.zeros_like(q)
    qbd_ref[0:qb, :] = jnp.where(lane < DA_DH, q, zero)
    qbd_ref[qb:2 * qb, :] = jnp.where(lane >= DA_DH, q, zero)
    m_ref[...] = jnp.full_like(m_ref, -jnp.inf)
    acc_ref[...] = jnp.zeros_like(acc_ref)

    def rows(j):
        if isinstance(j, int):
            return pl.ds(j * kb, kb)
        return pl.ds(pl.multiple_of(j * kb, kb), kb)

    def scores(s_ref, j):
        s_ref[...] = _dot_nt(qbd_ref[...], k_ref[0, rows(j), :])

    def update(s_ref, j):
        s = s_ref[...]
        m_prev = m_ref[...]
        m_new = jnp.maximum(m_prev, jnp.max(s, axis=1, keepdims=True))
        alpha = jnp.exp2(m_prev - m_new)
        p = jnp.exp2(s - jnp.tile(m_new, (1, kb // LANES)))
        v = v_ref[0, rows(j), :]
        v_ext = jnp.concatenate([v, jnp.ones_like(v)], axis=1)
        acc_ref[...] = jnp.tile(alpha, (1, 2)) * acc_ref[...] + _dot(p.astype(BF16), v_ext)
        m_ref[...] = m_new

    bufs = (s0_ref, s1_ref)
    scores(s0_ref, 0)

    def group(j0, last):
        for u in range(unroll):
            if not (last and u == unroll - 1):
                scores(bufs[(u + 1) % 2], j0 + u + 1)
            update(bufs[u % 2], j0 + u)

    def body(jj, carry):
        group(jj * unroll, False)
        return carry

    lax.fori_loop(0, nk // unroll - 1, body, 0)
    group(nk - unroll, True)

    lf = lam_ref[...]
    s01 = jnp.sum(lf[0:1] * lf[1:2], axis=1, keepdims=True)
    s23 = jnp.sum(lf[2:3] * lf[3:4], axis=1, keepdims=True)
    lam = jnp.exp(s01) - jnp.exp(s23) + lambda_init
    acc = acc_ref[...]
    on = acc[:, 0:LANES] / acc[:, LANES:2 * LANES]
    o = on[0:qb] - lam * on[qb:2 * qb]
    ms = jnp.mean(o * o, axis=-1, keepdims=True)
    y = o * lax.rsqrt(ms + SUBLN_EPS) * sg_ref[...] * (1.0 - lambda_init)
    o_ref[0] = (y * gate_ref[0].astype(F32)).astype(BF16)


def _diff_attn(dq, dk, dv, dgate, da_lambda, da_subln_g, *, layer, qb, kb):
    batch, seq_len, _ = dq.shape
    assert seq_len % (2 * kb) == 0 and seq_len % qb == 0
    nk = seq_len // kb
    unroll = 4 if nk % 4 == 0 else 2
    lambda_init = 0.8 - 0.6 * math.exp(-0.3 * layer)
    qspec = pl.BlockSpec((1, qb, LANES), lambda b, h, i: (b, i, h))
    kspec = pl.BlockSpec((1, seq_len, LANES), lambda b, h, i: (b, 0, h))
    return pl.pallas_call(
        functools.partial(_attn_kernel, qb=qb, kb=kb, unroll=unroll, lambda_init=lambda_init),
        out_shape=jax.ShapeDtypeStruct((batch, seq_len, DA_HEADS * 2 * DA_DH), BF16),
        grid=(batch, DA_HEADS, seq_len // qb),
        in_specs=[
            qspec, kspec, kspec, qspec,
            pl.BlockSpec((4, DA_DH), lambda b, h, i: (0, 0)),
            pl.BlockSpec((1, 2 * DA_DH), lambda b, h, i: (0, 0)),
        ],
        out_specs=qspec,
        scratch_shapes=[
            pltpu.VMEM((2 * qb, LANES), BF16),
            pltpu.VMEM((2 * qb, kb), F32),
            pltpu.VMEM((2 * qb, kb), F32),
            pltpu.VMEM((2 * qb, LANES), F32),
            pltpu.VMEM((2 * qb, 2 * LANES), F32),
        ],
        compiler_params=pltpu.CompilerParams(
            dimension_semantics=("parallel", "parallel", "parallel"),
            vmem_limit_bytes=VMEM_LIMIT),
        name="diff_attn",
    )(dq, dk, dv, dgate, da_lambda, da_subln_g.reshape(1, 2 * DA_DH))


def _out_kernel(x_ref, mod_ref, of_ref, ob_ref, hgate_ref, oda_ref, mgh_ref, mgd_ref,
                hg_g_ref, gpost_ref, wph_ref, wpd_ref, wo_ref, y_ref):
    g = hg_g_ref[...]
    gate = mod_ref[0, 2:3, :]
    for r in range(x_ref.shape[0] // IN_SUB_ROWS):
        rs = slice(r * IN_SUB_ROWS, (r + 1) * IN_SUB_ROWS)
        o = of_ref[rs, :] + ob_ref[rs, :]
        heads = []
        for h in range(HG_HEADS):
            oh = o[:, h * HG_DV:(h + 1) * HG_DV]
            ms = jnp.mean(oh * oh, axis=-1, keepdims=True)
            heads.append(oh * lax.rsqrt(ms + NORM_EPS) * g)
        o_hg = (jnp.concatenate(heads, axis=1) * hgate_ref[rs, :].astype(F32)).astype(BF16)
        merged = (mgh_ref[rs, :].astype(F32) * _dot(o_hg, wph_ref[...])
                  + mgd_ref[rs, :].astype(F32) * _dot(oda_ref[rs, :], wpd_ref[...]))
        out = _dot(merged.astype(BF16), wo_ref[...])
        ms = jnp.mean(out * out, axis=-1, keepdims=True)
        normed = out * lax.rsqrt(ms + NORM_EPS) * gpost_ref[...]
        y_ref[rs, :] = x_ref[rs, :] + gate * normed


def _out_stage(x2, mod, of, ob, hgate, oda, mgh, mgd, hg_norm_g, g_post, wph, wpd, wo,
               *, seq_len, tm):
    rows = x2.shape[0]
    nt = seq_len // tm
    row = lambda i: (i, 0)
    const2 = lambda i: (0, 0)
    seg = pl.BlockSpec((tm, SEG), row)
    wide = pl.BlockSpec((tm, D_MODEL), row)
    return pl.pallas_call(
        _out_kernel,
        out_shape=jax.ShapeDtypeStruct((rows, D_MODEL), F32),
        grid=(rows // tm,),
        in_specs=[
            wide,
            pl.BlockSpec((1, 3, D_MODEL), lambda i: (i // nt, 0, 0)),
            seg, seg, seg, seg, wide, wide,
            pl.BlockSpec((1, HG_DV), const2),
            pl.BlockSpec((1, D_MODEL), const2),
            pl.BlockSpec((SEG, D_MODEL), const2),
            pl.BlockSpec((SEG, D_MODEL), const2),
            pl.BlockSpec((D_MODEL, D_MODEL), const2),
        ],
        out_specs=wide,
        compiler_params=pltpu.CompilerParams(dimension_semantics=("parallel",),
                                             vmem_limit_bytes=VMEM_LIMIT),
        name="out_stage",
    )(x2, mod, of, ob, hgate, oda, mgh, mgd, hg_norm_g.reshape(1, HG_DV),
      g_post.reshape(1, D_MODEL), wph, wpd, wo)


def _tiles(seq_len):
    tm = min(512, seq_len)
    hg_rows = min(256, seq_len)
    qb = min(512, seq_len)
    kb = min(512, seq_len // 2)
    return tm, hg_rows, qb, kb


def _layer(x, mod, layer, w_in_bf16, lb_raw, g_pre, g_post, hg_norm_g, da_lambda, da_subln_g,
           wph, wpd, wo, rope_tabs):
    batch, seq_len, _ = x.shape
    tm, hg_rows, qb, kb = _tiles(seq_len)
    rows = batch * seq_len
    x2 = x.reshape(rows, D_MODEL)
    (hq, gf, gb, hv, hgate, dq, dk, dv, dgate, mgh, mgd) = _in_proj(
        x2, mod, g_pre, w_in_bf16, lb_raw, rope_tabs, layer=layer, seq_len=seq_len, tm=tm)
    r3 = lambda t: t.reshape(batch, seq_len, SEG)
    of, ob = _hgrn(r3(hq), r3(gf), r3(gb), r3(hv), block_rows=hg_rows)
    oda = _diff_attn(r3(dq), r3(dk), r3(dv), r3(dgate), da_lambda, da_subln_g,
                     layer=layer, qb=qb, kb=kb)
    y2 = _out_stage(x2, mod, of.reshape(rows, SEG), ob.reshape(rows, SEG), hgate,
                    oda.reshape(rows, SEG), mgh, mgd, hg_norm_g, g_post, wph, wpd, wo,
                    seq_len=seq_len, tm=tm)
    return y2.reshape(batch, seq_len, D_MODEL)


def kernel(x_prompt, x_sample, c_prompt, c_sample, w_ada, b_ada, g_pre, g_post, w_in, hg_lower_bounds, hg_norm_g, da_lambda, da_subln_g, w_proj_hg, w_proj_da, w_out):
    depth = w_in.shape[0]
    nb_prompt = x_prompt.shape[0]
    c_all = jnp.concatenate([c_prompt, c_sample], axis=0)
    mod_all = _ada_mod(c_all, w_ada, b_ada)
    lb_raw = hg_lower_bounds.astype(F32)
    w_in_bf16 = w_in.astype(BF16)
    wph_all = w_proj_hg.astype(BF16)
    wpd_all = w_proj_da.astype(BF16)
    wo_all = w_out.astype(BF16)
    tabs = {x.shape[1]: _rope_tables(x.shape[1]) for x in (x_prompt, x_sample)}

    ys = [x_prompt, x_sample]
    for l in range(depth):
        mod_l = mod_all[l].reshape(-1, 3, D_MODEL)
        mods = (mod_l[:nb_prompt], mod_l[nb_prompt:])
        for gi in range(2):
            x = ys[gi]
            ys[gi] = _layer(x, mods[gi], l, w_in_bf16[l], lb_raw, g_pre[l], g_post[l], hg_norm_g[l],
                            da_lambda[l], da_subln_g[l], wph_all[l], wpd_all[l], wo_all[l],
                            tabs[x.shape[1]])
    return (ys[0], ys[1])
```

```python
import functools
import math

import jax
import jax.numpy as jnp
import numpy as np
from jax import lax
from jax.experimental import pallas as pl
from jax.experimental.pallas import tpu as pltpu

F32 = jnp.float32
BF16 = jnp.bfloat16

D_MODEL = 1024
HG_HEADS = 4
HG_DK = 128
HG_DV = 128
DA_HEADS = 4
DA_DH = 64
ROPE_THETA = 500000.0
ROPE_DIM = DA_DH // 4
NORM_EPS = 1e-6
SUBLN_EPS = 1e-5
SEG = 512
N_SEG = 13
D_IN = SEG * N_SEG
LOG2_E = math.log2(math.e)
Q_SCALE = DA_DH ** -0.5 * LOG2_E

LANES = 128
IN_SUB_ROWS = 256
HG_CHUNK = 64
HG_SUB = 16
HG_MAX_FACTORED_LOG2_DECAY = 96.0
VMEM_LIMIT = 56 * 1024 * 1024


def _dot(a, b):
    return jnp.dot(a, b, preferred_element_type=F32)


def _dot_nt(a, b):
    return lax.dot_general(a, b, (((1,), (1,)), ((), ())), preferred_element_type=F32)


def _dot_tn(a, b):
    return lax.dot_general(a, b, (((0,), (0,)), ((), ())), preferred_element_type=F32)


def _silu(x):
    return x * jax.nn.sigmoid(x)


def _ada_kernel(c_ref, w_ref, b_ref, o_ref):
    c = c_ref[...]
    o_ref[0] = jnp.dot(_silu(c), w_ref[0], precision=lax.Precision.HIGHEST,
                       preferred_element_type=F32) + b_ref[0]


def _ada_mod(c_all, w_ada, b_ada):
    depth = w_ada.shape[0]
    nb = c_all.shape[0]
    tn = D_MODEL
    return pl.pallas_call(
        _ada_kernel,
        out_shape=jax.ShapeDtypeStruct((depth, nb, 3 * D_MODEL), F32),
        grid=(depth, 3 * D_MODEL // tn),
        in_specs=[
            pl.BlockSpec((nb, D_MODEL), lambda l, j: (0, 0)),
            pl.BlockSpec((1, D_MODEL, tn), lambda l, j: (l, 0, j)),
            pl.BlockSpec((1, 1, tn), lambda l, j: (l, 0, j)),
        ],
        out_specs=pl.BlockSpec((1, nb, tn), lambda l, j: (l, 0, j)),
        name="ada_mod",
    )(c_all, w_ada, b_ada.reshape(depth, 1, 3 * D_MODEL))


def _rope_tables(seq_len):
    half = ROPE_DIM // 2
    inv_freq = jnp.power(ROPE_THETA, -jnp.arange(0, ROPE_DIM, 2, dtype=F32) / ROPE_DIM)
    dim = np.arange(LANES) % DA_DH
    pos = jnp.arange(seq_len, dtype=jnp.int32).astype(F32)
    ang = pos[:, None] * inv_freq[dim % half][None, :]
    cos = jnp.where(dim < ROPE_DIM, jnp.cos(ang), 1.0)
    sin = jnp.sin(ang)
    sin_up = jnp.where(dim < half, -sin, 0.0)
    sin_dn = jnp.where((dim >= half) & (dim < ROPE_DIM), sin, 0.0)
    return cos, sin_up, sin_dn


def _inproj_kernel(x_ref, mod_ref, gpre_ref, w_ref, lbraw_ref, cos_ref, sup_ref, sdn_ref,
                   hq_ref, gf_ref, gb_ref, hv_ref, hgate_ref, dq_ref, dk_ref, dv_ref, dgate_ref,
                   mgh_ref, mgd_ref, *, layer):
    raw = lbraw_ref[...]
    e = jnp.exp(raw - jnp.max(raw, axis=0, keepdims=True))
    soft = e / jnp.sum(e, axis=0, keepdims=True)
    lb = jnp.sum(soft[0:layer + 1], axis=0) - soft[0]
    shift = mod_ref[0, 0:1, :]
    scale = mod_ref[0, 1:2, :]
    half = ROPE_DIM // 2

    def log2_forget(z, lb_dir):
        return jnp.log(lb_dir + (1.0 - lb_dir) * jax.nn.sigmoid(z)) * LOG2_E

    for r in range(x_ref.shape[0] // IN_SUB_ROWS):
        rs = slice(r * IN_SUB_ROWS, (r + 1) * IN_SUB_ROWS)
        x = x_ref[rs, :]
        ms = jnp.mean(x * x, axis=-1, keepdims=True)
        y = x * lax.rsqrt(ms + NORM_EPS) * gpre_ref[...]
        h = (y * (1.0 + scale) + shift).astype(BF16)
        cos = cos_ref[rs, :]
        sup = sup_ref[rs, :]
        sdn = sdn_ref[rs, :]

        def proj(c0, h=h):
            return _dot(h, w_ref[:, c0:c0 + SEG])

        def rope(z, out_scale, cos=cos, sup=sup, sdn=sdn):
            outs = []
            for c in range(SEG // LANES):
                t = z[:, c * LANES:(c + 1) * LANES]
                rot = t * cos + pltpu.roll(t, LANES - half, 1) * sup + pltpu.roll(t, half, 1) * sdn
                outs.append(rot * out_scale)
            return jnp.concatenate(outs, axis=1)

        hq_ref[rs, :] = _silu(proj(0 * SEG)).astype(BF16)
        gf_ref[rs, :] = log2_forget(proj(1 * SEG), lb[0:1])
        gb_ref[rs, :] = log2_forget(proj(2 * SEG), lb[1:2])
        hv_ref[rs, :] = proj(3 * SEG).astype(BF16)
        hgate_ref[rs, :] = _silu(proj(4 * SEG)).astype(BF16)
        dq_ref[rs, :] = rope(proj(5 * SEG), Q_SCALE).astype(BF16)
        dk_ref[rs, :] = rope(proj(6 * SEG), 1.0).astype(BF16)
        dv_ref[rs, :] = proj(7 * SEG).astype(BF16)
        dgate_ref[rs, :] = _silu(proj(8 * SEG)).astype(BF16)
        for c in range(D_MODEL // SEG):
            cs = slice(c * SEG, (c + 1) * SEG)
            mgh_ref[rs, cs] = jax.nn.sigmoid(proj(9 * SEG + c * SEG)).astype(BF16)
            mgd_ref[rs, cs] = jax.nn.sigmoid(proj(11 * SEG + c * SEG)).astype(BF16)


def _in_proj(x2, mod, g_pre, w_in_bf16, lb_raw, rope_tabs, *, layer, seq_len, tm):
    rows = x2.shape[0]
    assert tm % IN_SUB_ROWS == 0 and seq_len % tm == 0
    nt = seq_len // tm
    row = lambda i: (i, 0)
    const2 = lambda i: (0, 0)
    tab = pl.BlockSpec((tm, LANES), lambda i: (i % nt, 0))
    seg_bf16 = jax.ShapeDtypeStruct((rows, SEG), BF16)
    seg_f32 = jax.ShapeDtypeStruct((rows, SEG), F32)
    wide_bf16 = jax.ShapeDtypeStruct((rows, D_MODEL), BF16)
    seg_spec = pl.BlockSpec((tm, SEG), row)
    wide_spec = pl.BlockSpec((tm, D_MODEL), row)
    return pl.pallas_call(
        functools.partial(_inproj_kernel, layer=layer),
        out_shape=(seg_bf16, seg_f32, seg_f32, seg_bf16, seg_bf16, seg_bf16, seg_bf16, seg_bf16,
                   seg_bf16, wide_bf16, wide_bf16),
        grid=(rows // tm,),
        in_specs=[
            pl.BlockSpec((tm, D_MODEL), row),
            pl.BlockSpec((1, 3, D_MODEL), lambda i: (i // nt, 0, 0)),
            pl.BlockSpec((1, D_MODEL), const2),
            pl.BlockSpec((D_MODEL, D_IN), const2, pipeline_mode=pl.Buffered(1)),
            pl.BlockSpec(lb_raw.shape, lambda i: (0, 0, 0)),
            tab, tab, tab,
        ],
        out_specs=(seg_spec,) * 9 + (wide_spec, wide_spec),
        compiler_params=pltpu.CompilerParams(dimension_semantics=("parallel",),
                                             vmem_limit_bytes=VMEM_LIMIT),
        name="in_proj",
    )(x2, mod, g_pre.reshape(1, D_MODEL), w_in_bf16, lb_raw, *rope_tabs)


def _cumsum_rows(tri_bf16, g):
    hi = g.astype(BF16)
    r1 = g - hi.astype(F32)
    mid = r1.astype(BF16)
    lo = (r1 - mid.astype(F32)).astype(BF16)
    return _dot(tri_bf16, hi) + _dot(tri_bf16, mid) + _dot(tri_bf16, lo)


def _hgrn_head_matmuls(q, k, v, a, a_end, st_ref, head, fwd, exact_diag):
    C, SUB = HG_CHUNK, HG_SUB
    st = st_ref[head]
    o_inter = _dot_nt((q * jnp.exp2(a)).astype(BF16), st.astype(BF16))
    kd = (k * jnp.exp2(a_end - a)).astype(BF16)
    st_ref[head] = st * jnp.exp2(a_end) + _dot_tn(v, kd)

    nsub = C // SUB
    blocks = []
    for i in range(nsub):
        lo, hi = i * SUB, (i + 1) * SUB
        ai, qi = a[lo:hi], q[lo:hi]
        if fwd:
            ref = a[lo - 1:lo] if i > 0 else jnp.zeros((1, LANES), F32)
            k_lo, k_hi = 0, (lo if exact_diag else hi)
        else:
            ref = a[hi:hi + 1] if i < nsub - 1 else jnp.zeros((1, LANES), F32)
            k_lo, k_hi = (hi if exact_diag else lo), C
        if k_hi > k_lo:
            kt = (k[k_lo:k_hi] * jnp.exp2(ref - a[k_lo:k_hi])).astype(BF16)
            pads = [jnp.zeros((n, LANES), BF16) for n in (k_lo, C - k_hi)]
            kt = jnp.concatenate([p for p in (pads[0], kt, pads[1]) if p.shape[0]], axis=0)
            sc = _dot_nt((qi * jnp.exp2(ai - ref)).astype(BF16), kt)
        else:
            sc = jnp.zeros((SUB, C), F32)
        blocks.append(sc)
    return o_inter, blocks


def _hgrn_head_finish(q, v, a, o_inter, blocks, ak_ref, head, r0, fwd, exact_diag):
    C, SUB = HG_CHUNK, HG_SUB
    HALF = SUB // 2
    lane = lax.broadcasted_iota(jnp.int32, (HALF, C), 1)
    lane2 = lax.broadcasted_iota(jnp.int32, (SUB, C), 1)
    trow = lax.broadcasted_iota(jnp.int32, (SUB, C), 0)
    masked = []
    for i, sc in enumerate(blocks):
        lo, hi = i * SUB, (i + 1) * SUB
        ai, qi = a[lo:hi], q[lo:hi]
        sc_a, sc_b = sc[0:HALF], sc[HALF:SUB]
        qa, qb_, aa, ab = qi[0:HALF], qi[HALF:SUB], ai[0:HALF], ai[HALF:SUB]
        for s in range(SUB if exact_diag else 0):
            a_s = ak_ref[0, head, r0 + lo + s:r0 + lo + s + 1, :]
            k_s = ak_ref[1, head, r0 + lo + s:r0 + lo + s + 1, :]
            need_a = (s < HALF) if fwd else True
            need_b = True if fwd else (s >= HALF)
            if need_a:
                col = jnp.sum(qa * jnp.exp2(aa - a_s) * k_s, axis=1, keepdims=True)
                sc_a = jnp.where(lane == lo + s, col, sc_a)
            if need_b:
                col = jnp.sum(qb_ * jnp.exp2(ab - a_s) * k_s, axis=1, keepdims=True)
                sc_b = jnp.where(lane == lo + s, col, sc_b)
        sc = jnp.concatenate([sc_a, sc_b], axis=0)
        seen = (lane2 <= trow + lo) if fwd else (lane2 >= trow + lo)
        masked.append(jnp.where(seen, sc, 0.0))
    scores = jnp.concatenate(masked, axis=0).astype(BF16)
    return o_inter + _dot(scores, v)


def _hgrn_block(fwd_refs, bwd_refs, nch, exact_diag):
    C = HG_CHUNK
    T = nch * C
    ti = lax.broadcasted_iota(jnp.int32, (T, T), 0)
    si = lax.broadcasted_iota(jnp.int32, (T, T), 1)
    same_chunk = (ti // C) == (si // C)
    scans = ((fwd_refs, True), (bwd_refs, False))
    for (q_ref, g_ref, v_ref, o_ref, st_ref, ak_ref), fwd in scans:
        g = g_ref[0]
        tri = jnp.logical_and(same_chunk, (si <= ti) if fwd else (si >= ti))
        a_all = _cumsum_rows(tri.astype(BF16), g)
        k_all = 1.0 - jnp.exp2(g)
        for h in range(HG_HEADS):
            cols = slice(h * HG_DK, (h + 1) * HG_DK)
            ak_ref[0, h] = a_all[:, cols]
            ak_ref[1, h] = k_all[:, cols]

    def issue(step):
        pending = []
        for (q_ref, g_ref, v_ref, o_ref, st_ref, ak_ref), fwd in scans:
            r0 = (step if fwd else nch - 1 - step) * C
            rows = slice(r0, r0 + C)
            for h in range(HG_HEADS):
                cols = slice(h * HG_DK, (h + 1) * HG_DK)
                a = ak_ref[0, h, rows, :]
                k = ak_ref[1, h, rows, :]
                a_end = a[C - 1:C] if fwd else a[0:1]
                q = q_ref[0, rows, cols].astype(F32)
                v = v_ref[0, rows, cols]
                o_inter, blocks = _hgrn_head_matmuls(q, k, v, a, a_end, st_ref, h, fwd, exact_diag)
                pending.append((q, v, a, o_inter, blocks, ak_ref, h, fwd, o_ref, r0, cols))
        return pending

    def finish(pending):
        for q, v, a, o_inter, blocks, ak_ref, h, fwd, o_ref, r0, cols in pending:
            o = _hgrn_head_finish(q, v, a, o_inter, blocks, ak_ref, h, r0, fwd, exact_diag)
            o_ref[0, r0:r0 + C, cols] = o.astype(o_ref.dtype)

    pending = issue(0)
    for step in range(nch):
        following = issue(step + 1) if step + 1 < nch else []
        finish(pending)
        pending = following


def _min_subblock_log2_decay(g_ref):
    g = g_ref[0]
    rows, width = g.shape
    sums = jnp.sum(g.reshape(rows // HG_SUB, HG_SUB, width), axis=1)
    return jnp.min(sums)


def _hgrn_kernel(qf_ref, gf_ref, vf_ref, qb_ref, gb_ref, vb_ref, of_ref, ob_ref, sf_ref, sb_ref,
                 akf_ref, akb_ref, *, block_rows):
    @pl.when(pl.program_id(1) == 0)
    def _():
        sf_ref[...] = jnp.zeros_like(sf_ref)
        sb_ref[...] = jnp.zeros_like(sb_ref)

    nch = block_rows // HG_CHUNK

    def run(exact_diag):
        _hgrn_block((qf_ref, gf_ref, vf_ref, of_ref, sf_ref, akf_ref),
                    (qb_ref, gb_ref, vb_ref, ob_ref, sb_ref, akb_ref), nch, exact_diag)

    worst = jnp.minimum(_min_subblock_log2_decay(gf_ref), _min_subblock_log2_decay(gb_ref))
    mild = worst >= -HG_MAX_FACTORED_LOG2_DECAY

    @pl.when(mild)
    def _():
        run(False)

    @pl.when(jnp.logical_not(mild))
    def _():
        run(True)


def _hgrn(hq, gf, gb, hv, *, block_rows):
    batch, seq_len, width = hq.shape
    nblk = seq_len // block_rows
    fwd = pl.BlockSpec((1, block_rows, width), lambda b, j: (b, j, 0))
    bwd = pl.BlockSpec((1, block_rows, width), lambda b, j: (b, nblk - 1 - j, 0))
    out = jax.ShapeDtypeStruct((batch, seq_len, HG_HEADS * HG_DV), BF16)
    state = pltpu.VMEM((HG_HEADS, HG_DV, HG_DK), F32)
    staged = pltpu.VMEM((2, HG_HEADS, block_rows, HG_DK), F32)
    return pl.pallas_call(
        functools.partial(_hgrn_kernel, block_rows=block_rows),
        out_shape=(out, out),
        grid=(batch, nblk),
        in_specs=[fwd, fwd, fwd, bwd, bwd, bwd],
        out_specs=(fwd, bwd),
        scratch_shapes=[state, state, staged, staged],
        compiler_params=pltpu.CompilerParams(
            dimension_semantics=("parallel", "arbitrary"),
            vmem_limit_bytes=VMEM_LIMIT),
        name="hgrn2",
    )(hq, gf, hv, hq, gb, hv)


def _attn_kernel(q_ref, k_ref, v_ref, gate_ref, lam_ref, sg_ref, o_ref,
                 qbd_ref, s0_ref, s1_ref, m_ref, acc_ref, *, qb, kb, unroll, lambda_init):
    nk = k_ref.shape[1] // kb
    q = q_ref[0]
    lane = lax.broadcasted_iota(jnp.int32, q.shape, 1)
    zero = jnp.zeros_like(q)
    qbd_ref[0:qb, :] = jnp.where(lane < DA_DH, q, zero)
    qbd_ref[qb:2 * qb, :] = jnp.where(lane >= DA_DH, q, zero)
    m_ref[...] = jnp.full_like(m_ref, -jnp.inf)
    acc_ref[...] = jnp.zeros_like(acc_ref)

    def rows(j):
        if isinstance(j, int):
            return pl.ds(j * kb, kb)
        return pl.ds(pl.multiple_of(j * kb, kb), kb)

    def scores(s_ref, j):
        s_ref[...] = _dot_nt(qbd_ref[...], k_ref[0, rows(j), :])

    def update(s_ref, j):
        s = s_ref[...]
        m_prev = m_ref[...]
        m_new = jnp.maximum(m_prev, jnp.max(s, axis=1, keepdims=True))
        alpha = jnp.exp2(m_prev - m_new)
        p = jnp.exp2(s - jnp.tile(m_new, (1, kb // LANES)))
        v = v_ref[0, rows(j), :]
        v_ext = jnp.concatenate([v, jnp.ones_like(v)], axis=1)
        acc_ref[...] = jnp.tile(alpha, (1, 2)) * acc_ref[...] + _dot(p.astype(BF16), v_ext)
        m_ref[...] = m_new

    bufs = (s0_ref, s1_ref)
    scores(s0_ref, 0)

    def group(j0, last):
        for u in range(unroll):
            if not (last and u == unroll - 1):
                scores(bufs[(u + 1) % 2], j0 + u + 1)
            update(bufs[u % 2], j0 + u)

    def body(jj, carry):
        group(jj * unroll, False)
        return carry

    lax.fori_loop(0, nk // unroll - 1, body, 0)
    group(nk - unroll, True)

    lf = lam_ref[...]
    s01 = jnp.sum(lf[0:1] * lf[1:2], axis=1, keepdims=True)
    s23 = jnp.sum(lf[2:3] * lf[3:4], axis=1, keepdims=True)
    lam = jnp.exp(s01) - jnp.exp(s23) + lambda_init
    acc = acc_ref[...]
    on = acc[:, 0:LANES] / acc[:, LANES:2 * LANES]
    o = on[0:qb] - lam * on[qb:2 * qb]
    ms = jnp.mean(o * o, axis=-1, keepdims=True)
    y = o * lax.rsqrt(ms + SUBLN_EPS) * sg_ref[...] * (1.0 - lambda_init)
    o_ref[0] = (y * gate_ref[0].astype(F32)).astype(BF16)


def _diff_attn(dq, dk, dv, dgate, da_lambda, da_subln_g, *, layer, qb, kb):
    batch, seq_len, _ = dq.shape
    assert seq_len % (2 * kb) == 0 and seq_len % qb == 0
    nk = seq_len // kb
    unroll = max(u for u in (2, 4, 8) if nk % u == 0)
    lambda_init = 0.8 - 0.6 * math.exp(-0.3 * layer)
    qspec = pl.BlockSpec((1, qb, LANES), lambda b, h, i: (b, i, h))
    kspec = pl.BlockSpec((1, seq_len, LANES), lambda b, h, i: (b, 0, h))
    return pl.pallas_call(
        functools.partial(_attn_kernel, qb=qb, kb=kb, unroll=unroll, lambda_init=lambda_init),
        out_shape=jax.ShapeDtypeStruct((batch, seq_len, DA_HEADS * 2 * DA_DH), BF16),
        grid=(batch, DA_HEADS, seq_len // qb),
        in_specs=[
            qspec, kspec, kspec, qspec,
            pl.BlockSpec((4, DA_DH), lambda b, h, i: (0, 0)),
            pl.BlockSpec((1, 2 * DA_DH), lambda b, h, i: (0, 0)),
        ],
        out_specs=qspec,
        scratch_shapes=[
            pltpu.VMEM((2 * qb, LANES), BF16),
            pltpu.VMEM((2 * qb, kb), F32),
            pltpu.VMEM((2 * qb, kb), F32),
            pltpu.VMEM((2 * qb, LANES), F32),
            pltpu.VMEM((2 * qb, 2 * LANES), F32),
        ],
        compiler_params=pltpu.CompilerParams(
            dimension_semantics=("parallel", "parallel", "parallel"),
            vmem_limit_bytes=VMEM_LIMIT),
        name="diff_attn",
    )(dq, dk, dv, dgate, da_lambda, da_subln_g.reshape(1, 2 * DA_DH))


def _out_kernel(x_ref, mod_ref, of_ref, ob_ref, hgate_ref, oda_ref, mgh_ref, mgd_ref,
                hg_g_ref, gpost_ref, wph_ref, wpd_ref, wo_ref, y_ref):
    g = hg_g_ref[...]
    gate = mod_ref[0, 2:3, :]
    for r in range(x_ref.shape[0] // IN_SUB_ROWS):
        rs = slice(r * IN_SUB_ROWS, (r + 1) * IN_SUB_ROWS)
        o = of_ref[rs, :].astype(F32) + ob_ref[rs, :].astype(F32)
        heads = []
        for h in range(HG_HEADS):
            oh = o[:, h * HG_DV:(h + 1) * HG_DV]
            ms = jnp.mean(oh * oh, axis=-1, keepdims=True)
            heads.append(oh * lax.rsqrt(ms + NORM_EPS) * g)
        o_hg = (jnp.concatenate(heads, axis=1) * hgate_ref[rs, :].astype(F32)).astype(BF16)
        merged = (mgh_ref[rs, :].astype(F32) * _dot(o_hg, wph_ref[...])
                  + mgd_ref[rs, :].astype(F32) * _dot(oda_ref[rs, :], wpd_ref[...]))
        out = _dot(merged.astype(BF16), wo_ref[...])
        ms = jnp.mean(out * out, axis=-1, keepdims=True)
        normed = out * lax.rsqrt(ms + NORM_EPS) * gpost_ref[...]
        y_ref[rs, :] = x_ref[rs, :] + gate * normed


def _out_stage(x2, mod, of, ob, hgate, oda, mgh, mgd, hg_norm_g, g_post, wph, wpd, wo,
               *, seq_len, tm):
    rows = x2.shape[0]
    nt = seq_len // tm
    row = lambda i: (i, 0)
    const2 = lambda i: (0, 0)
    seg = pl.BlockSpec((tm, SEG), row)
    wide = pl.BlockSpec((tm, D_MODEL), row)
    return pl.pallas_call(
        _out_kernel,
        out_shape=jax.ShapeDtypeStruct((rows, D_MODEL), F32),
        grid=(rows // tm,),
        in_specs=[
            wide,
            pl.BlockSpec((1, 3, D_MODEL), lambda i: (i // nt, 0, 0)),
            seg, seg, seg, seg, wide, wide,
            pl.BlockSpec((1, HG_DV), const2),
            pl.BlockSpec((1, D_MODEL), const2),
            pl.BlockSpec((SEG, D_MODEL), const2),
            pl.BlockSpec((SEG, D_MODEL), const2),
            pl.BlockSpec((D_MODEL, D_MODEL), const2),
        ],
        out_specs=wide,
        compiler_params=pltpu.CompilerParams(dimension_semantics=("parallel",),
                                             vmem_limit_bytes=VMEM_LIMIT),
        name="out_stage",
    )(x2, mod, of, ob, hgate, oda, mgh, mgd, hg_norm_g.reshape(1, HG_DV),
      g_post.reshape(1, D_MODEL), wph, wpd, wo)


def _tiles(seq_len):
    tm = min(512, seq_len)
    hg_rows = min(256, seq_len)
    qb = min(512, seq_len)
    kb = min(512, seq_len // 2)
    return tm, hg_rows, qb, kb


def _layer(x, mod, layer, w_in_bf16, lb_raw, g_pre, g_post, hg_norm_g, da_lambda, da_subln_g,
           wph, wpd, wo, rope_tabs):
    batch, seq_len, _ = x.shape
    tm, hg_rows, qb, kb = _tiles(seq_len)
    rows = batch * seq_len
    x2 = x.reshape(rows, D_MODEL)
    (hq, gf, gb, hv, hgate, dq, dk, dv, dgate, mgh, mgd) = _in_proj(
        x2, mod, g_pre, w_in_bf16, lb_raw, rope_tabs, layer=layer, seq_len=seq_len, tm=tm)
    r3 = lambda t: t.reshape(batch, seq_len, SEG)
    of, ob = _hgrn(r3(hq), r3(gf), r3(gb), r3(hv), block_rows=hg_rows)
    oda = _diff_attn(r3(dq), r3(dk), r3(dv), r3(dgate), da_lambda, da_subln_g,
                     layer=layer, qb=qb, kb=kb)
    y2 = _out_stage(x2, mod, of.reshape(rows, SEG), ob.reshape(rows, SEG), hgate,
                    oda.reshape(rows, SEG), mgh, mgd, hg_norm_g, g_post, wph, wpd, wo,
                    seq_len=seq_len, tm=tm)
    return y2.reshape(batch, seq_len, D_MODEL)


def kernel(x_prompt, x_sample, c_prompt, c_sample, w_ada, b_ada, g_pre, g_post, w_in, hg_lower_bounds, hg_norm_g, da_lambda, da_subln_g, w_proj_hg, w_proj_da, w_out):
    depth = w_in.shape[0]
    nb_prompt = x_prompt.shape[0]
    c_all = jnp.concatenate([c_prompt, c_sample], axis=0)
    mod_all = _ada_mod(c_all, w_ada, b_ada)
    lb_raw = hg_lower_bounds.astype(F32)
    w_in_bf16 = w_in.astype(BF16)
    wph_all = w_proj_hg.astype(BF16)
    wpd_all = w_proj_da.astype(BF16)
    wo_all = w_out.astype(BF16)
    tabs = {x.shape[1]: _rope_tables(x.shape[1]) for x in (x_prompt, x_sample)}

    ys = [x_prompt, x_sample]
    for l in range(depth):
        mod_l = mod_all[l].reshape(-1, 3, D_MODEL)
        mods = (mod_l[:nb_prompt], mod_l[nb_prompt:])
        for gi in range(2):
            x = ys[gi]
            ys[gi] = _layer(x, mods[gi], l, w_in_bf16[l], lb_raw, g_pre[l], g_post[l], hg_norm_g[l],
                            da_lambda[l], da_subln_g[l], wph_all[l], wpd_all[l], wo_all[l],
                            tabs[x.shape[1]])
    return (ys[0], ys[1])
```

```python
import functools
import math

import jax
import jax.numpy as jnp
import numpy as np
from jax import lax
from jax.experimental import pallas as pl
from jax.experimental.pallas import tpu as pltpu

F32 = jnp.float32
BF16 = jnp.bfloat16

D_MODEL = 1024
HG_HEADS = 4
HG_DK = 128
HG_DV = 128
DA_HEADS = 4
DA_DH = 64
ROPE_THETA = 500000.0
ROPE_DIM = DA_DH // 4
NORM_EPS = 1e-6
SUBLN_EPS = 1e-5
SEG = 512
N_SEG = 13
D_IN = SEG * N_SEG
LOG2_E = math.log2(math.e)
Q_SCALE = DA_DH ** -0.5 * LOG2_E

LANES = 128
IN_SUB_ROWS = 256
HG_CHUNK = 64
HG_SUB = 16
HG_CUMSUM_ROWS = 256
HG_MAX_FACTORED_LOG2_DECAY = 96.0
VMEM_LIMIT = 56 * 1024 * 1024


def _dot(a, b):
    return jnp.dot(a, b, preferred_element_type=F32)


def _dot_nt(a, b):
    return lax.dot_general(a, b, (((1,), (1,)), ((), ())), preferred_element_type=F32)


def _dot_tn(a, b):
    return lax.dot_general(a, b, (((0,), (0,)), ((), ())), preferred_element_type=F32)


def _silu(x):
    return x * jax.nn.sigmoid(x)


def _ada_kernel(c_ref, w_ref, b_ref, o_ref):
    c = c_ref[...]
    o_ref[0] = jnp.dot(_silu(c), w_ref[0], precision=lax.Precision.HIGHEST,
                       preferred_element_type=F32) + b_ref[0]


def _ada_mod(c_all, w_ada, b_ada):
    depth = w_ada.shape[0]
    nb = c_all.shape[0]
    tn = D_MODEL
    return pl.pallas_call(
        _ada_kernel,
        out_shape=jax.ShapeDtypeStruct((depth, nb, 3 * D_MODEL), F32),
        grid=(depth, 3 * D_MODEL // tn),
        in_specs=[
            pl.BlockSpec((nb, D_MODEL), lambda l, j: (0, 0)),
            pl.BlockSpec((1, D_MODEL, tn), lambda l, j: (l, 0, j)),
            pl.BlockSpec((1, 1, tn), lambda l, j: (l, 0, j)),
        ],
        out_specs=pl.BlockSpec((1, nb, tn), lambda l, j: (l, 0, j)),
        name="ada_mod",
    )(c_all, w_ada, b_ada.reshape(depth, 1, 3 * D_MODEL))


def _rope_tables(seq_len):
    half = ROPE_DIM // 2
    inv_freq = jnp.power(ROPE_THETA, -jnp.arange(0, ROPE_DIM, 2, dtype=F32) / ROPE_DIM)
    dim = np.arange(LANES) % DA_DH
    pos = jnp.arange(seq_len, dtype=jnp.int32).astype(F32)
    ang = pos[:, None] * inv_freq[dim % half][None, :]
    cos = jnp.where(dim < ROPE_DIM, jnp.cos(ang), 1.0)
    sin = jnp.sin(ang)
    sin_up = jnp.where(dim < half, -sin, 0.0)
    sin_dn = jnp.where((dim >= half) & (dim < ROPE_DIM), sin, 0.0)
    return cos, sin_up, sin_dn


def _inproj_kernel(x_ref, mod_ref, gpre_ref, w_ref, lbraw_ref, cos_ref, sup_ref, sdn_ref,
                   hq_ref, gf_ref, gb_ref, hv_ref, hgate_ref, dq_ref, dk_ref, dv_ref, dgate_ref,
                   mgh_ref, mgd_ref, *, layer):
    raw = lbraw_ref[...]
    e = jnp.exp(raw - jnp.max(raw, axis=0, keepdims=True))
    soft = e / jnp.sum(e, axis=0, keepdims=True)
    lb = jnp.sum(soft[0:layer + 1], axis=0) - soft[0]
    shift = mod_ref[0, 0:1, :]
    scale = mod_ref[0, 1:2, :]
    half = ROPE_DIM // 2

    def log2_forget(z, lb_dir):
        return jnp.log(lb_dir + (1.0 - lb_dir) * jax.nn.sigmoid(z)) * LOG2_E

    for r in range(x_ref.shape[0] // IN_SUB_ROWS):
        rs = slice(r * IN_SUB_ROWS, (r + 1) * IN_SUB_ROWS)
        x = x_ref[rs, :]
        ms = jnp.mean(x * x, axis=-1, keepdims=True)
        y = x * lax.rsqrt(ms + NORM_EPS) * gpre_ref[...]
        h = (y * (1.0 + scale) + shift).astype(BF16)
        cos = cos_ref[rs, :]
        sup = sup_ref[rs, :]
        sdn = sdn_ref[rs, :]

        def proj(c0, h=h):
            return _dot(h, w_ref[:, c0:c0 + SEG])

        def rope(z, out_scale, cos=cos, sup=sup, sdn=sdn):
            outs = []
            for c in range(SEG // LANES):
                t = z[:, c * LANES:(c + 1) * LANES]
                rot = t * cos + pltpu.roll(t, LANES - half, 1) * sup + pltpu.roll(t, half, 1) * sdn
                outs.append(rot * out_scale)
            return jnp.concatenate(outs, axis=1)

        hq_ref[rs, :] = _silu(proj(0 * SEG)).astype(BF16)
        gf_ref[rs, :] = log2_forget(proj(1 * SEG), lb[0:1])
        gb_ref[rs, :] = log2_forget(proj(2 * SEG), lb[1:2])
        hv_ref[rs, :] = proj(3 * SEG).astype(BF16)
        hgate_ref[rs, :] = _silu(proj(4 * SEG)).astype(BF16)
        dq_ref[rs, :] = rope(proj(5 * SEG), Q_SCALE).astype(BF16)
        dk_ref[rs, :] = rope(proj(6 * SEG), 1.0).astype(BF16)
        dv_ref[rs, :] = proj(7 * SEG).astype(BF16)
        dgate_ref[rs, :] = _silu(proj(8 * SEG)).astype(BF16)
        for c in range(D_MODEL // SEG):
            cs = slice(c * SEG, (c + 1) * SEG)
            mgh_ref[rs, cs] = jax.nn.sigmoid(proj(9 * SEG + c * SEG)).astype(BF16)
            mgd_ref[rs, cs] = jax.nn.sigmoid(proj(11 * SEG + c * SEG)).astype(BF16)


def _in_proj(x2, mod, g_pre, w_in_bf16, lb_raw, rope_tabs, *, layer, seq_len, tm):
    rows = x2.shape[0]
    assert tm % IN_SUB_ROWS == 0 and seq_len % tm == 0
    nt = seq_len // tm
    row = lambda i: (i, 0)
    const2 = lambda i: (0, 0)
    tab = pl.BlockSpec((tm, LANES), lambda i: (i % nt, 0))
    seg_bf16 = jax.ShapeDtypeStruct((rows, SEG), BF16)
    seg_f32 = jax.ShapeDtypeStruct((rows, SEG), F32)
    wide_bf16 = jax.ShapeDtypeStruct((rows, D_MODEL), BF16)
    seg_spec = pl.BlockSpec((tm, SEG), row)
    wide_spec = pl.BlockSpec((tm, D_MODEL), row)
    return pl.pallas_call(
        functools.partial(_inproj_kernel, layer=layer),
        out_shape=(seg_bf16, seg_f32, seg_f32, seg_bf16, seg_bf16, seg_bf16, seg_bf16, seg_bf16,
                   seg_bf16, wide_bf16, wide_bf16),
        grid=(rows // tm,),
        in_specs=[
            pl.BlockSpec((tm, D_MODEL), row),
            pl.BlockSpec((1, 3, D_MODEL), lambda i: (i // nt, 0, 0)),
            pl.BlockSpec((1, D_MODEL), const2),
            pl.BlockSpec((D_MODEL, D_IN), const2, pipeline_mode=pl.Buffered(1)),
            pl.BlockSpec(lb_raw.shape, lambda i: (0, 0, 0)),
            tab, tab, tab,
        ],
        out_specs=(seg_spec,) * 9 + (wide_spec, wide_spec),
        compiler_params=pltpu.CompilerParams(dimension_semantics=("parallel",),
                                             vmem_limit_bytes=VMEM_LIMIT),
        name="in_proj",
    )(x2, mod, g_pre.reshape(1, D_MODEL), w_in_bf16, lb_raw, *rope_tabs)


def _cumsum_rows(tri_bf16, g):
    hi = g.astype(BF16)
    r1 = g - hi.astype(F32)
    mid = r1.astype(BF16)
    lo = (r1 - mid.astype(F32)).astype(BF16)
    return _dot(tri_bf16, hi) + _dot(tri_bf16, mid) + _dot(tri_bf16, lo)


def _hgrn_head_matmuls(q, k, v, a, a_end, st_ref, head, fwd, exact_diag):
    C, SUB = HG_CHUNK, HG_SUB
    st = st_ref[head]
    o_inter = _dot_nt((q * jnp.exp2(a)).astype(BF16), st.astype(BF16))
    kd = (k * jnp.exp2(a_end - a)).astype(BF16)
    st_ref[head] = st * jnp.exp2(a_end) + _dot_tn(v, kd)

    nsub = C // SUB
    blocks = []
    for i in range(nsub):
        lo, hi = i * SUB, (i + 1) * SUB
        ai, qi = a[lo:hi], q[lo:hi]
        if fwd:
            ref = a[lo - 1:lo] if i > 0 else jnp.zeros((1, LANES), F32)
            k_lo, k_hi = 0, (lo if exact_diag else hi)
        else:
            ref = a[hi:hi + 1] if i < nsub - 1 else jnp.zeros((1, LANES), F32)
            k_lo, k_hi = (hi if exact_diag else lo), C
        if k_hi > k_lo:
            kt = (k[k_lo:k_hi] * jnp.exp2(ref - a[k_lo:k_hi])).astype(BF16)
            pads = [jnp.zeros((n, LANES), BF16) for n in (k_lo, C - k_hi)]
            kt = jnp.concatenate([p for p in (pads[0], kt, pads[1]) if p.shape[0]], axis=0)
            sc = _dot_nt((qi * jnp.exp2(ai - ref)).astype(BF16), kt)
        else:
            sc = jnp.zeros((SUB, C), F32)
        blocks.append(sc)
    return o_inter, blocks


def _hgrn_head_finish(q, v, a, o_inter, blocks, ak_ref, head, r0, fwd, exact_diag):
    C, SUB = HG_CHUNK, HG_SUB
    HALF = SUB // 2
    lane = lax.broadcasted_iota(jnp.int32, (HALF, C), 1)
    lane2 = lax.broadcasted_iota(jnp.int32, (SUB, C), 1)
    trow = lax.broadcasted_iota(jnp.int32, (SUB, C), 0)
    masked = []
    for i, sc in enumerate(blocks):
        lo, hi = i * SUB, (i + 1) * SUB
        ai, qi = a[lo:hi], q[lo:hi]
        sc_a, sc_b = sc[0:HALF], sc[HALF:SUB]
        qa, qb_, aa, ab = qi[0:HALF], qi[HALF:SUB], ai[0:HALF], ai[HALF:SUB]
        for s in range(SUB if exact_diag else 0):
            a_s = ak_ref[0, head, r0 + lo + s:r0 + lo + s + 1, :]
            k_s = ak_ref[1, head, r0 + lo + s:r0 + lo + s + 1, :]
            need_a = (s < HALF) if fwd else True
            need_b = True if fwd else (s >= HALF)
            if need_a:
                col = jnp.sum(qa * jnp.exp2(aa - a_s) * k_s, axis=1, keepdims=True)
                sc_a = jnp.where(lane == lo + s, col, sc_a)
            if need_b:
                col = jnp.sum(qb_ * jnp.exp2(ab - a_s) * k_s, axis=1, keepdims=True)
                sc_b = jnp.where(lane == lo + s, col, sc_b)
        sc = jnp.concatenate([sc_a, sc_b], axis=0)
        seen = (lane2 <= trow + lo) if fwd else (lane2 >= trow + lo)
        masked.append(jnp.where(seen, sc, 0.0))
    scores = jnp.concatenate(masked, axis=0).astype(BF16)
    return o_inter + _dot(scores, v)


def _hgrn_block(fwd_refs, bwd_refs, nch, exact_diag):
    C = HG_CHUNK
    P = min(HG_CUMSUM_ROWS, nch * C)
    ti = lax.broadcasted_iota(jnp.int32, (P, P), 0)
    si = lax.broadcasted_iota(jnp.int32, (P, P), 1)
    same_chunk = (ti // C) == (si // C)
    scans = ((fwd_refs, True), (bwd_refs, False))
    for (q_ref, g_ref, v_ref, o_ref, st_ref, ak_ref), fwd in scans:
        tri = jnp.logical_and(same_chunk, (si <= ti) if fwd else (si >= ti)).astype(BF16)
        for p0 in range(0, nch * C, P):
            g = g_ref[0, p0:p0 + P, :]
            a_all = _cumsum_rows(tri, g)
            k_all = 1.0 - jnp.exp2(g)
            for h in range(HG_HEADS):
                cols = slice(h * HG_DK, (h + 1) * HG_DK)
                ak_ref[0, h, p0:p0 + P, :] = a_all[:, cols]
                ak_ref[1, h, p0:p0 + P, :] = k_all[:, cols]

    def issue(step):
        pending = []
        for (q_ref, g_ref, v_ref, o_ref, st_ref, ak_ref), fwd in scans:
            r0 = (step if fwd else nch - 1 - step) * C
            rows = slice(r0, r0 + C)
            for h in range(HG_HEADS):
                cols = slice(h * HG_DK, (h + 1) * HG_DK)
                a = ak_ref[0, h, rows, :]
                k = ak_ref[1, h, rows, :]
                a_end = a[C - 1:C] if fwd else a[0:1]
                q = q_ref[0, rows, cols].astype(F32)
                v = v_ref[0, rows, cols]
                o_inter, blocks = _hgrn_head_matmuls(q, k, v, a, a_end, st_ref, h, fwd, exact_diag)
                pending.append((q, v, a, o_inter, blocks, ak_ref, h, fwd, o_ref, r0, cols))
        return pending

    def finish(pending):
        for q, v, a, o_inter, blocks, ak_ref, h, fwd, o_ref, r0, cols in pending:
            o = _hgrn_head_finish(q, v, a, o_inter, blocks, ak_ref, h, r0, fwd, exact_diag)
            o_ref[0, r0:r0 + C, cols] = o.astype(o_ref.dtype)

    pending = issue(0)
    for step in range(nch):
        following = issue(step + 1) if step + 1 < nch else []
        finish(pending)
        pending = following


def _min_subblock_log2_decay(g_ref):
    g = g_ref[0]
    rows, width = g.shape
    sums = jnp.sum(g.reshape(rows // HG_SUB, HG_SUB, width), axis=1)
    return jnp.min(sums)


def _hgrn_kernel(qf_ref, gf_ref, vf_ref, qb_ref, gb_ref, vb_ref, of_ref, ob_ref, sf_ref, sb_ref,
                 akf_ref, akb_ref, *, block_rows):
    @pl.when(pl.program_id(1) == 0)
    def _():
        sf_ref[...] = jnp.zeros_like(sf_ref)
        sb_ref[...] = jnp.zeros_like(sb_ref)

    nch = block_rows // HG_CHUNK

    def run(exact_diag):
        _hgrn_block((qf_ref, gf_ref, vf_ref, of_ref, sf_ref, akf_ref),
                    (qb_ref, gb_ref, vb_ref, ob_ref, sb_ref, akb_ref), nch, exact_diag)

    worst = jnp.minimum(_min_subblock_log2_decay(gf_ref), _min_subblock_log2_decay(gb_ref))
    mild = worst >= -HG_MAX_FACTORED_LOG2_DECAY

    @pl.when(mild)
    def _():
        run(False)

    @pl.when(jnp.logical_not(mild))
    def _():
        run(True)


def _hgrn(hq, gf, gb, hv, *, block_rows):
    batch, seq_len, width = hq.shape
    nblk = seq_len // block_rows
    fwd = pl.BlockSpec((1, block_rows, width), lambda b, j: (b, j, 0))
    bwd = pl.BlockSpec((1, block_rows, width), lambda b, j: (b, nblk - 1 - j, 0))
    out = jax.ShapeDtypeStruct((batch, seq_len, HG_HEADS * HG_DV), BF16)
    state = pltpu.VMEM((HG_HEADS, HG_DV, HG_DK), F32)
    staged = pltpu.VMEM((2, HG_HEADS, block_rows, HG_DK), F32)
    return pl.pallas_call(
        functools.partial(_hgrn_kernel, block_rows=block_rows),
        out_shape=(out, out),
        grid=(batch, nblk),
        in_specs=[fwd, fwd, fwd, bwd, bwd, bwd],
        out_specs=(fwd, bwd),
        scratch_shapes=[state, state, staged, staged],
        compiler_params=pltpu.CompilerParams(
            dimension_semantics=("parallel", "arbitrary"),
            vmem_limit_bytes=VMEM_LIMIT),
        name="hgrn2",
    )(hq, gf, hv, hq, gb, hv)


def _attn_kernel(q_ref, k_ref, v_ref, gate_ref, lam_ref, sg_ref, o_ref,
                 qbd_ref, s0_ref, s1_ref, m_ref, acc_ref, *, qb, kb, unroll, lambda_init):
    nk = k_ref.shape[1] // kb
    q = q_ref[0]
    lane = lax.broadcasted_iota(jnp.int32, q.shape, 1)
    zero = jnp.zeros_like(q)
    qbd_ref[0:qb, :] = jnp.where(lane < DA_DH, q, zero)
    qbd_ref[qb:2 * qb, :] = jnp.where(lane >= DA_DH, q, zero)
    m_ref[...] = jnp.full_like(m_ref, -jnp.inf)
    acc_ref[...] = jnp.zeros_like(acc_ref)

    def rows(j):
        if isinstance(j, int):
            return pl.ds(j * kb, kb)
        return pl.ds(pl.multiple_of(j * kb, kb), kb)

    def scores(s_ref, j):
        s_ref[...] = _dot_nt(qbd_ref[...], k_ref[0, rows(j), :])

    def update(s_ref, j):
        s = s_ref[...]
        m_prev = m_ref[...]
        m_new = jnp.maximum(m_prev, jnp.max(s, axis=1, keepdims=True))
        alpha = jnp.exp2(m_prev - m_new)
        p = jnp.exp2(s - jnp.tile(m_new, (1, kb // LANES)))
        v = v_ref[0, rows(j), :]
        v_ext = jnp.concatenate([v, jnp.ones_like(v)], axis=1)
        acc_ref[...] = jnp.tile(alpha, (1, 2)) * acc_ref[...] + _dot(p.astype(BF16), v_ext)
        m_ref[...] = m_new

    bufs = (s0_ref, s1_ref)
    scores(s0_ref, 0)

    def group(j0, last):
        for u in range(unroll):
            if not (last and u == unroll - 1):
                scores(bufs[(u + 1) % 2], j0 + u + 1)
            update(bufs[u % 2], j0 + u)

    def body(jj, carry):
        group(jj * unroll, False)
        return carry

    lax.fori_loop(0, nk // unroll - 1, body, 0)
    group(nk - unroll, True)

    lf = lam_ref[...]
    s01 = jnp.sum(lf[0:1] * lf[1:2], axis=1, keepdims=True)
    s23 = jnp.sum(lf[2:3] * lf[3:4], axis=1, keepdims=True)
    lam = jnp.exp(s01) - jnp.exp(s23) + lambda_init
    acc = acc_ref[...]
    on = acc[:, 0:LANES] / acc[:, LANES:2 * LANES]
    o = on[0:qb] - lam * on[qb:2 * qb]
    ms = jnp.mean(o * o, axis=-1, keepdims=True)
    y = o * lax.rsqrt(ms + SUBLN_EPS) * sg_ref[...] * (1.0 - lambda_init)
    o_ref[0] = (y * gate_ref[0].astype(F32)).astype(BF16)


def _diff_attn(dq, dk, dv, dgate, da_lambda, da_subln_g, *, layer, qb, kb):
    batch, seq_len, _ = dq.shape
    assert seq_len % (2 * kb) == 0 and seq_len % qb == 0
    nk = seq_len // kb
    unroll = max(u for u in (2, 4, 8) if nk % u == 0)
    lambda_init = 0.8 - 0.6 * math.exp(-0.3 * layer)
    qspec = pl.BlockSpec((1, qb, LANES), lambda b, h, i: (b, i, h))
    kspec = pl.BlockSpec((1, seq_len, LANES), lambda b, h, i: (b, 0, h))
    return pl.pallas_call(
        functools.partial(_attn_kernel, qb=qb, kb=kb, unroll=unroll, lambda_init=lambda_init),
        out_shape=jax.ShapeDtypeStruct((batch, seq_len, DA_HEADS * 2 * DA_DH), BF16),
        grid=(batch, DA_HEADS, seq_len // qb),
        in_specs=[
            qspec, kspec, kspec, qspec,
            pl.BlockSpec((4, DA_DH), lambda b, h, i: (0, 0)),
            pl.BlockSpec((1, 2 * DA_DH), lambda b, h, i: (0, 0)),
        ],
        out_specs=qspec,
        scratch_shapes=[
            pltpu.VMEM((2 * qb, LANES), BF16),
            pltpu.VMEM((2 * qb, kb), F32),
            pltpu.VMEM((2 * qb, kb), F32),
            pltpu.VMEM((2 * qb, LANES), F32),
            pltpu.VMEM((2 * qb, 2 * LANES), F32),
        ],
        compiler_params=pltpu.CompilerParams(
            dimension_semantics=("parallel", "parallel", "parallel"),
            vmem_limit_bytes=VMEM_LIMIT),
        name="diff_attn",
    )(dq, dk, dv, dgate, da_lambda, da_subln_g.reshape(1, 2 * DA_DH))


def _out_kernel(x_ref, mod_ref, of_ref, ob_ref, hgate_ref, oda_ref, mgh_ref, mgd_ref,
                hg_g_ref, gpost_ref, wph_ref, wpd_ref, wo_ref, y_ref):
    g = hg_g_ref[...]
    gate = mod_ref[0, 2:3, :]
    for r in range(x_ref.shape[0] // IN_SUB_ROWS):
        rs = slice(r * IN_SUB_ROWS, (r + 1) * IN_SUB_ROWS)
        o = of_ref[rs, :].astype(F32) + ob_ref[rs, :].astype(F32)
        heads = []
        for h in range(HG_HEADS):
            oh = o[:, h * HG_DV:(h + 1) * HG_DV]
            ms = jnp.mean(oh * oh, axis=-1, keepdims=True)
            heads.append(oh * lax.rsqrt(ms + NORM_EPS) * g)
        o_hg = (jnp.concatenate(heads, axis=1) * hgate_ref[rs, :].astype(F32)).astype(BF16)
        merged = (mgh_ref[rs, :].astype(F32) * _dot(o_hg, wph_ref[...])
                  + mgd_ref[rs, :].astype(F32) * _dot(oda_ref[rs, :], wpd_ref[...]))
        out = _dot(merged.astype(BF16), wo_ref[...])
        ms = jnp.mean(out * out, axis=-1, keepdims=True)
        normed = out * lax.rsqrt(ms + NORM_EPS) * gpost_ref[...]
        y_ref[rs, :] = x_ref[rs, :] + gate * normed


def _out_stage(x2, mod, of, ob, hgate, oda, mgh, mgd, hg_norm_g, g_post, wph, wpd, wo,
               *, seq_len, tm):
    rows = x2.shape[0]
    nt = seq_len // tm
    row = lambda i: (i, 0)
    const2 = lambda i: (0, 0)
    seg = pl.BlockSpec((tm, SEG), row)
    wide = pl.BlockSpec((tm, D_MODEL), row)
    return pl.pallas_call(
        _out_kernel,
        out_shape=jax.ShapeDtypeStruct((rows, D_MODEL), F32),
        grid=(rows // tm,),
        in_specs=[
            wide,
            pl.BlockSpec((1, 3, D_MODEL), lambda i: (i // nt, 0, 0)),
            seg, seg, seg, seg, wide, wide,
            pl.BlockSpec((1, HG_DV), const2),
            pl.BlockSpec((1, D_MODEL), const2),
            pl.BlockSpec((SEG, D_MODEL), const2),
            pl.BlockSpec((SEG, D_MODEL), const2),
            pl.BlockSpec((D_MODEL, D_MODEL), const2),
        ],
        out_specs=wide,
        compiler_params=pltpu.CompilerParams(dimension_semantics=("parallel",),
                                             vmem_limit_bytes=VMEM_LIMIT),
        name="out_stage",
    )(x2, mod, of, ob, hgate, oda, mgh, mgd, hg_norm_g.reshape(1, HG_DV),
      g_post.reshape(1, D_MODEL), wph, wpd, wo)


def _tiles(seq_len):
    tm = min(512, seq_len)
    hg_rows = min(512, seq_len)
    qb = min(512, seq_len)
    kb = min(512, seq_len // 2)
    return tm, hg_rows, qb, kb


def _layer(x, mod, layer, w_in_bf16, lb_raw, g_pre, g_post, hg_norm_g, da_lambda, da_subln_g,
           wph, wpd, wo, rope_tabs):
    batch, seq_len, _ = x.shape
    tm, hg_rows, qb, kb = _tiles(seq_len)
    rows = batch * seq_len
    x2 = x.reshape(rows, D_MODEL)
    (hq, gf, gb, hv, hgate, dq, dk, dv, dgate, mgh, mgd) = _in_proj(
        x2, mod, g_pre, w_in_bf16, lb_raw, rope_tabs, layer=layer, seq_len=seq_len, tm=tm)
    r3 = lambda t: t.reshape(batch, seq_len, SEG)
    of, ob = _hgrn(r3(hq), r3(gf), r3(gb), r3(hv), block_rows=hg_rows)
    oda = _diff_attn(r3(dq), r3(dk), r3(dv), r3(dgate), da_lambda, da_subln_g,
                     layer=layer, qb=qb, kb=kb)
    y2 = _out_stage(x2, mod, of.reshape(rows, SEG), ob.reshape(rows, SEG), hgate,
                    oda.reshape(rows, SEG), mgh, mgd, hg_norm_g, g_post, wph, wpd, wo,
                    seq_len=seq_len, tm=tm)
    return y2.reshape(batch, seq_len, D_MODEL)


def kernel(x_prompt, x_sample, c_prompt, c_sample, w_ada, b_ada, g_pre, g_post, w_in, hg_lower_bounds, hg_norm_g, da_lambda, da_subln_g, w_proj_hg, w_proj_da, w_out):
    depth = w_in.shape[0]
    nb_prompt = x_prompt.shape[0]
    c_all = jnp.concatenate([c_prompt, c_sample], axis=0)
    mod_all = _ada_mod(c_all, w_ada, b_ada)
    lb_raw = hg_lower_bounds.astype(F32)
    w_in_bf16 = w_in.astype(BF16)
    wph_all = w_proj_hg.astype(BF16)
    wpd_all = w_proj_da.astype(BF16)
    wo_all = w_out.astype(BF16)
    tabs = {x.shape[1]: _rope_tables(x.shape[1]) for x in (x_prompt, x_sample)}

    ys = [x_prompt, x_sample]
    for l in range(depth):
        mod_l = mod_all[l].reshape(-1, 3, D_MODEL)
        mods = (mod_l[:nb_prompt], mod_l[nb_prompt:])
        for gi in range(2):
            x = ys[gi]
            ys[gi] = _layer(x, mods[gi], l, w_in_bf16[l], lb_raw, g_pre[l], g_post[l], hg_norm_g[l],
                            da_lambda[l], da_subln_g[l], wph_all[l], wpd_all[l], wo_all[l],
                            tabs[x.shape[1]])
    return (ys[0], ys[1])
```
